```python
import math
import jax, jax.numpy as jnp
from jax import lax
import numpy as np

D_MODEL = 1024
BATCH = 8
SEQ = 8192
DEPTH = 4

N_META = 16
BLOCK = 128
META_PAD = BLOCK - N_META
RET_HEADS = 4
RET_DK = D_MODEL // 8
RET_DV = D_MODEL // 4
DIFF_HEADS = 8
DIFF_HD = D_MODEL // 16
DIFF_VD = 2 * DIFF_HD
CONV_WIDTH = D_MODEL
CONV_K = 3
N_BRANCH = 3
BRANCH_W = D_MODEL
D_FF = 4 * D_MODEL
EPS = 1e-6
NEG_INF = -1e30
IN_WIDTHS = (RET_HEADS * RET_DK, RET_HEADS * RET_DK, RET_HEADS * RET_DV, RET_HEADS * RET_DV,
             2 * DIFF_HEADS * DIFF_HD, 2 * DIFF_HEADS * DIFF_HD, DIFF_HEADS * DIFF_VD,
             CONV_WIDTH, CONV_WIDTH, CONV_WIDTH, N_BRANCH * D_MODEL)
D_IN = sum(IN_WIDTHS)

kernel_name = "hybrid_retention_diffattn_shortconv_block"


def rms_norm(x, w):
    xf = x.astype(jnp.float32)
    y = xf * lax.rsqrt(jnp.mean(xf * xf, axis=-1, keepdims=True) + EPS)
    return (y * w.astype(jnp.float32)).astype(x.dtype)


def head_layer_norm(y):
    yf = y.astype(jnp.float32)
    mu = jnp.mean(yf, axis=-1, keepdims=True)
    var = jnp.mean(jnp.square(yf - mu), axis=-1, keepdims=True)
    return ((yf - mu) * lax.rsqrt(var + EPS)).astype(y.dtype)


def pad_front(t, n):
    return jnp.pad(t, [(0, 0), (n, 0)] + [(0, 0)] * (t.ndim - 2))


def column_bounds():
    bounds, start = [], 0
    for w in IN_WIDTHS:
        bounds.append((start, start + w))
        start += w
    return bounds


def retention(q, k, v):
    b = q.shape[0]
    dt = q.dtype
    k = k * (RET_DK ** -0.5)
    q, k, v = (pad_front(t, META_PAD) for t in (q, k, v))
    nc = q.shape[1] // BLOCK

    def chunks(t):
        return t.reshape(b, nc, BLOCK, RET_HEADS, t.shape[-1]).transpose(1, 0, 3, 2, 4)

    log_g = jnp.log1p(-jnp.exp2(-5.0 - jnp.arange(RET_HEADS, dtype=jnp.float32)))
    i = jnp.arange(BLOCK, dtype=jnp.float32)
    dist = i[:, None] - i[None, :]
    intra = jnp.where(dist >= 0, jnp.exp(log_g[:, None, None] * jnp.maximum(dist, 0.0)), 0.0).astype(dt)
    q_decay = jnp.exp(log_g[:, None] * (i + 1.0)).astype(dt)
    k_decay = jnp.exp(log_g[:, None] * (BLOCK - 1.0 - i)).astype(dt)
    s_decay = jnp.exp(log_g * BLOCK).astype(dt)

    def step(state, qkv):
        qc, kc, vc = qkv
        scores = jnp.einsum('bhid,bhjd->bhij', qc, kc) * intra
        out = (jnp.einsum('bhij,bhjv->bhiv', scores, vc)
               + jnp.einsum('bhid,bhdv->bhiv', qc * q_decay[:, :, None], state))
        state = state * s_decay[:, None, None] + jnp.einsum('bhjd,bhjv->bhdv', kc * k_decay[:, :, None], vc)
        return state, out

    s0 = jnp.zeros((b, RET_HEADS, RET_DK, RET_DV), dt)
    _, out = lax.scan(step, s0, (chunks(q), chunks(k), chunks(v)))
    out = out.transpose(1, 0, 3, 2, 4).reshape(b, nc * BLOCK, RET_HEADS, RET_DV)
    return out[:, META_PAD:]


def diff_attention(q, k, v, lam, subln_w, lam_init):
    b = q.shape[0]
    q, k, v = (pad_front(t, META_PAD) for t in (q, k, v))
    p = q.shape[1]
    nb = p // BLOCK
    kt = k.transpose(0, 2, 3, 1, 4)
    vt = v.transpose(0, 2, 1, 3)
    qb = q.reshape(b, nb, BLOCK, DIFF_HEADS, 2, DIFF_HD).transpose(1, 0, 3, 4, 2, 5)
    slopes = jnp.exp2(-8.0 / DIFF_HEADS * (jnp.arange(DIFF_HEADS, dtype=jnp.float32) + 1.0))
    kpos = jnp.arange(p)
    scale = DIFF_HD ** -0.5

    def block(args):
        qblk, bi = args
        qpos = bi * BLOCK + jnp.arange(BLOCK)
        dist = (qpos[:, None] - kpos[None, :]).astype(jnp.float32)
        valid = (dist >= 0) & (kpos[None, :] >= META_PAD)
        logits = (jnp.einsum('bhmqd,bhmkd->bhmqk', qblk, kt, preferred_element_type=jnp.float32) * scale
                  - slopes[:, None, None, None] * dist)
        probs = jax.nn.softmax(jnp.where(valid, logits, NEG_INF), axis=-1)
        attn = probs[:, :, 0] - lam * probs[:, :, 1]
        return jnp.einsum('bhqk,bhkv->bhqv', attn.astype(vt.dtype), vt)

    out = lax.map(block, (qb, jnp.arange(nb)))
    out = out.transpose(1, 0, 3, 2, 4).reshape(b, p, DIFF_HEADS, DIFF_VD)[:, META_PAD:]
    return rms_norm(out, subln_w) * (1.0 - lam_init)


def short_conv(u, w):
    return lax.conv_general_dilated(
        u, w[:, None, :].astype(u.dtype), window_strides=(1,), padding=[(CONV_K - 1, 0)],
        dimension_numbers=('NWC', 'WIO', 'NWC'), feature_group_count=u.shape[-1])


def hybrid_layer(x, layer, norm1_w, w_in, conv_w, lam_vecs, subln_w, w_branch, w_out, norm2_w, w_up, w_down):
    b, l, _ = x.shape
    h = rms_norm(x, norm1_w)
    (rq, rk, rv, rg, dq, dk, dv, cb, cc, cx, gates) = [h @ w_in[:, s:e] for s, e in column_bounds()]

    ret = retention(rq.reshape(b, l, RET_HEADS, RET_DK), rk.reshape(b, l, RET_HEADS, RET_DK),
                    rv.reshape(b, l, RET_HEADS, RET_DV))
    ret = head_layer_norm(ret).reshape(b, l, BRANCH_W) * jax.nn.silu(rg)

    lam_init = 0.8 - 0.6 * math.exp(-0.3 * layer)
    lv = lam_vecs.astype(jnp.float32)
    lam = jnp.exp(jnp.sum(lv[0] * lv[1])) - jnp.exp(jnp.sum(lv[2] * lv[3])) + lam_init
    diff = diff_attention(dq.reshape(b, l, DIFF_HEADS, 2, DIFF_HD), dk.reshape(b, l, DIFF_HEADS, 2, DIFF_HD),
                          dv.reshape(b, l, DIFF_HEADS, DIFF_VD), lam, subln_w, lam_init).reshape(b, l, BRANCH_W)

    conv = cb * short_conv(cc * cx, conv_w)

    g = jax.nn.sigmoid(gates.astype(jnp.float32)).astype(x.dtype).reshape(b, l, N_BRANCH, D_MODEL)
    merged = (g[:, :, 0] * (ret @ w_branch[0]) + g[:, :, 1] * (diff @ w_branch[1])
              + g[:, :, 2] * (conv @ w_branch[2]))
    x = x + merged @ w_out

    u = jnp.square(jax.nn.relu(rms_norm(x, norm2_w) @ w_up))
    return x + u @ w_down


def setup_inputs(seed: int = 0) -> dict:
    key = jax.random.key(seed)
    ks = jax.random.split(key, 13)
    nrm = jax.random.normal
    f32 = jnp.float32
    return {
        "x": nrm(ks[0], (BATCH, SEQ, D_MODEL), f32),
        "meta_tokens": nrm(ks[1], (N_META, D_MODEL), f32),
        "norm1_w": 1.0 + 0.02 * nrm(ks[2], (DEPTH, D_MODEL), f32),
        "w_in": nrm(ks[3], (DEPTH, D_MODEL, D_IN), f32) * D_MODEL ** -0.5,
        "conv_w": nrm(ks[4], (DEPTH, CONV_K, CONV_WIDTH), f32) * CONV_K ** -0.5,
        "diff_lambda": 0.1 * nrm(ks[5], (DEPTH, 4, DIFF_HD), f32),
        "diff_subln_w": 1.0 + 0.02 * nrm(ks[6], (DEPTH, DIFF_VD), f32),
        "w_branch": nrm(ks[7], (DEPTH, N_BRANCH, BRANCH_W, D_MODEL), f32) * BRANCH_W ** -0.5,
        "w_out": nrm(ks[8], (DEPTH, D_MODEL, D_MODEL), f32) * D_MODEL ** -0.5,
        "norm2_w": 1.0 + 0.02 * nrm(ks[9], (DEPTH, D_MODEL), f32),
        "w_up": nrm(ks[10], (DEPTH, D_MODEL, D_FF), f32) * D_MODEL ** -0.5,
        "w_down": nrm(ks[11], (DEPTH, D_FF, D_MODEL), f32) * D_FF ** -0.5,
        "final_norm_w": 1.0 + 0.02 * nrm(ks[12], (D_MODEL,), f32),
    }


def reference(x, meta_tokens, norm1_w, w_in, conv_w, diff_lambda, diff_subln_w, w_branch, w_out,
              norm2_w, w_up, w_down, final_norm_w):
    b = x.shape[0]
    meta = jnp.broadcast_to(meta_tokens.astype(x.dtype)[None], (b, N_META, D_MODEL))
    h = jnp.concatenate([meta, x], axis=1)
    for layer in range(DEPTH):
        h = hybrid_layer(h, layer, norm1_w[layer], w_in[layer], conv_w[layer], diff_lambda[layer],
                         diff_subln_w[layer], w_branch[layer], w_out[layer], norm2_w[layer],
                         w_up[layer], w_down[layer])
    return rms_norm(h[:, N_META:], final_norm_w)
```

```python
import functools
import math

import jax
import jax.numpy as jnp
from jax import lax
from jax.experimental import pallas as pl
from jax.experimental.pallas import tpu as pltpu

N_META = 16
BLOCK = 128
META_PAD = BLOCK - N_META
RET_HEADS = 4
RET_DK = 128
RET_DV = 256
DIFF_HEADS = 8
DIFF_HD = 64
DIFF_VD = 128
CONV_K = 3
N_BRANCH = 3
EPS = 1e-6
NEG_INF = -1e30

D_MODEL = RET_HEADS * RET_DV
COL_RQ, COL_RK, COL_RV, COL_RG = 0, 512, 1024, 2048
COL_DQ, COL_DK, COL_DV = 3072, 4096, 5120
COL_CB, COL_CC, COL_CX, COL_GATE = 6144, 7168, 8192, 9216
D_IN = COL_GATE + N_BRANCH * D_MODEL

V7X_VMEM_LIMIT_BYTES = 56 * 1024 * 1024
BF16_SUBLANES = 16

F32 = jnp.float32
BF16 = jnp.bfloat16


def _largest_divisor(n, candidates):
    for c in candidates:
        if n % c == 0:
            return c
    raise ValueError(f"no tile in {candidates} divides {n}")


def _params(*semantics):
    return pltpu.CompilerParams(dimension_semantics=semantics, vmem_limit_bytes=V7X_VMEM_LIMIT_BYTES)


def _rms(x, w):
    return x * lax.rsqrt(jnp.mean(x * x, axis=-1, keepdims=True) + EPS) * w


def _inproj_kernel(x_ref, nw_ref, w_ref, o_ref, xn_ref):
    @pl.when(pl.program_id(1) == 0)
    def _():
        xn_ref[...] = _rms(x_ref[...], nw_ref[...]).astype(BF16)

    o_ref[...] = jnp.dot(xn_ref[...], w_ref[...], preferred_element_type=F32).astype(o_ref.dtype)


def _inproj(h, norm_w, w_in):
    n, d = h.shape
    tm = _largest_divisor(n, (1024, 512, 256, 128))
    tn = 2048
    return pl.pallas_call(
        _inproj_kernel,
        grid=(n // tm, D_IN // tn),
        in_specs=[
            pl.BlockSpec((tm, d), lambda i, j: (i, 0)),
            pl.BlockSpec((1, d), lambda i, j: (0, 0)),
            pl.BlockSpec((d, tn), lambda i, j: (0, j)),
        ],
        out_specs=pl.BlockSpec((tm, tn), lambda i, j: (i, j)),
        out_shape=jax.ShapeDtypeStruct((n, D_IN), BF16),
        scratch_shapes=[pltpu.VMEM((tm, d), BF16)],
        compiler_params=_params("parallel", "arbitrary"),
        name="inproj",
    )(h, norm_w, w_in)


def _ret_log_gamma(head):
    return math.log1p(-(2.0 ** (-5.0 - head)))


def _retention_kernel(q_ref, k_ref, v_ref, g_ref, o_ref, state_ref, intra_ref, qd_ref, kd_ref, *, chunks):
    @pl.when(pl.program_id(1) == 0)
    def _():
        state_ref[...] = jnp.zeros_like(state_ref)
        scale = RET_DK ** -0.5
        i_sq = lax.broadcasted_iota(jnp.int32, (BLOCK, BLOCK), 0).astype(F32)
        j_sq = lax.broadcasted_iota(jnp.int32, (BLOCK, BLOCK), 1).astype(F32)
        i_dv = lax.broadcasted_iota(jnp.int32, (BLOCK, RET_DV), 0).astype(F32)
        dist = i_sq - j_sq
        for h in range(RET_HEADS):
            lg = _ret_log_gamma(h)
            intra_ref[h] = jnp.where(dist >= 0, jnp.exp(lg * jnp.maximum(dist, 0.0)), 0.0) * scale
            qd_ref[h] = jnp.exp(lg * (i_dv + 1.0))
            kd_ref[h] = jnp.exp(lg * (BLOCK - 1.0 - i_sq)) * scale

    for c in range(chunks):
        rows = pl.ds(c * BLOCK, BLOCK)
        for h in range(RET_HEADS):
            s_decay = math.exp(_ret_log_gamma(h) * BLOCK)
            q = q_ref[rows, h * RET_DK:(h + 1) * RET_DK]
            k = k_ref[rows, h * RET_DK:(h + 1) * RET_DK]
            v = v_ref[rows, h * RET_DV:(h + 1) * RET_DV]
            st = state_ref[h]
            scores = lax.dot_general(q, k, (((1,), (1,)), ((), ())), preferred_element_type=F32)
            scores = (scores * intra_ref[h]).astype(BF16)
            out = (jnp.dot(scores, v, preferred_element_type=F32)
                   + qd_ref[h] * jnp.dot(q, st.astype(BF16), preferred_element_type=F32))
            kdec = (k.astype(F32) * kd_ref[h]).astype(BF16)
            state_ref[h] = st * s_decay + lax.dot_general(
                kdec, v, (((0,), (0,)), ((), ())), preferred_element_type=F32)
            mu = jnp.mean(out, axis=-1, keepdims=True)
            cen = out - mu
            var = jnp.mean(cen * cen, axis=-1, keepdims=True)
            g = g_ref[rows, h * RET_DV:(h + 1) * RET_DV].astype(F32)
            silu = g / (1.0 + jnp.exp(-g))
            o_ref[rows, h * RET_DV:(h + 1) * RET_DV] = (cen * lax.rsqrt(var + EPS) * silu).astype(o_ref.dtype)


def _retention(proj, batch, p_rows):
    n = proj.shape[0]
    nc = p_rows // BLOCK
    chunks = _largest_divisor(nc, (5, 3, 1))
    steps = nc // chunks
    tr = chunks * BLOCK
    qk_w = RET_HEADS * RET_DK

    def rows(b, c):
        return b * steps + c

    return pl.pallas_call(
        functools.partial(_retention_kernel, chunks=chunks),
        grid=(batch, steps),
        in_specs=[
            pl.BlockSpec((tr, qk_w), lambda b, c: (rows(b, c), COL_RQ // qk_w)),
            pl.BlockSpec((tr, qk_w), lambda b, c: (rows(b, c), COL_RK // qk_w)),
            pl.BlockSpec((tr, D_MODEL), lambda b, c: (rows(b, c), COL_RV // D_MODEL)),
            pl.BlockSpec((tr, D_MODEL), lambda b, c: (rows(b, c), COL_RG // D_MODEL)),
        ],
        out_specs=pl.BlockSpec((tr, D_MODEL), lambda b, c: (rows(b, c), 0)),
        out_shape=jax.ShapeDtypeStruct((n, D_MODEL), BF16),
        scratch_shapes=[
            pltpu.VMEM((RET_HEADS, RET_DK, RET_DV), F32),
            pltpu.VMEM((RET_HEADS, BLOCK, BLOCK), F32),
            pltpu.VMEM((RET_HEADS, BLOCK, RET_DV), F32),
            pltpu.VMEM((RET_HEADS, BLOCK, BLOCK), F32),
        ],
        compiler_params=_params("parallel", "arbitrary"),
        name="retention",
    )(proj, proj, proj, proj)


def _attn_kernel(q_ref, k_ref, v_ref, lamv_ref, lami_ref, sw_ref, o_ref,
                 vt_ref, qcat_ref, bias_ref, m_ref, l_ref, acc_ref, *, tq, n_kblocks):
    head = pl.program_id(1)
    qi = pl.program_id(2)
    kb_per_q = tq // BLOCK
    q0 = qi * tq
    lanes = 2 * tq

    @pl.when(qi == 0)
    def _():
        def xpose(j, carry):
            vb = v_ref[pl.ds(pl.multiple_of(j * BLOCK, BLOCK), BLOCK), :]
            vt_ref[j] = vb.astype(F32).T.astype(BF16)
            return carry
        lax.fori_loop(0, n_kblocks, xpose, 0)

    q = q_ref[...].astype(F32) * (DIFF_HD ** -0.5)
    lane = lax.broadcasted_iota(jnp.int32, (tq, 2 * DIFF_HD), 1)
    qcat_ref[0:tq, :] = jnp.where(lane < DIFF_HD, q, 0.0).astype(BF16)
    qcat_ref[tq:lanes, :] = jnp.where(lane >= DIFF_HD, q, 0.0).astype(BF16)

    slope = jnp.exp2(-(jnp.zeros((1, lanes), F32) + head.astype(F32) + 1.0))
    krel = lax.broadcasted_iota(jnp.int32, (BLOCK, lanes), 0)
    qrel = lax.broadcasted_iota(jnp.int32, (BLOCK, lanes), 1)
    qrel = jnp.where(qrel >= tq, qrel - tq, qrel)
    bias_ref[...] = (qrel - krel).astype(F32) * slope

    m_ref[...] = jnp.full_like(m_ref, NEG_INF)
    l_ref[...] = jnp.zeros_like(l_ref)
    acc_ref[...] = jnp.zeros_like(acc_ref)

    def step(j, masked):
        k0 = j * BLOCK
        kb = k_ref[pl.ds(pl.multiple_of(k0, BLOCK), BLOCK), :]
        s = lax.dot_general(kb, qcat_ref[...], (((1,), (1,)), ((), ())), preferred_element_type=F32)
        s = s - (bias_ref[...] + slope * (q0 - k0).astype(F32))
        if masked:
            dist = (qrel + q0) - (krel + k0)
            valid = (dist >= 0) & ((krel + k0) >= META_PAD)
            s = jnp.where(valid, s, NEG_INF)
        m_old = m_ref[...]
        m_new = jnp.maximum(m_old, jnp.max(s, axis=0, keepdims=True))
        alpha = jnp.exp(m_old - m_new)
        p = jnp.exp(s - m_new)
        l_ref[...] = alpha * l_ref[...] + jnp.sum(p, axis=0, keepdims=True)
        m_ref[...] = m_new
        pv = jnp.dot(vt_ref[j], p.astype(BF16), preferred_element_type=F32)
        acc_ref[...] = acc_ref[...] * alpha + pv

    first_diag = qi * kb_per_q

    @pl.when(qi > 0)
    def _():
        step(0, True)

    def body(j, carry):
        step(j, False)
        return carry
    lax.fori_loop(1, first_diag, body, 0)

    for d in range(kb_per_q):
        step(first_diag + d, True)

    lv = lamv_ref[...]
    lam = (jnp.exp(jnp.sum(lv[0:1] * lv[1:2], axis=-1, keepdims=True))
           - jnp.exp(jnp.sum(lv[2:3] * lv[3:4], axis=-1, keepdims=True)) + lami_ref[...])
    o = acc_ref[...] / l_ref[...]
    dlt = o[:, 0:tq] - lam * o[:, tq:lanes]
    ms = jnp.mean(dlt * dlt, axis=0, keepdims=True)
    y = dlt * lax.rsqrt(ms + EPS) * sw_ref[...] * (1.0 - lami_ref[...])
    o_ref[...] = y.T.astype(o_ref.dtype)


def _diff_attention(proj, lam_vecs, lam_init, subln_w, batch, p_rows):
    n = proj.shape[0]
    tq = _largest_divisor(p_rows, (256, 128))
    nq = p_rows // tq
    n_kblocks = p_rows // BLOCK
    hw = 2 * DIFF_HD
    return pl.pallas_call(
        functools.partial(_attn_kernel, tq=tq, n_kblocks=n_kblocks),
        grid=(batch, DIFF_HEADS, nq),
        in_specs=[
            pl.BlockSpec((tq, hw), lambda b, h, i: (b * nq + i, COL_DQ // hw + h)),
            pl.BlockSpec((p_rows, hw), lambda b, h, i: (b, COL_DK // hw + h)),
            pl.BlockSpec((p_rows, DIFF_VD), lambda b, h, i: (b, COL_DV // DIFF_VD + h)),
            pl.BlockSpec((4, DIFF_HD), lambda b, h, i: (0, 0)),
            pl.BlockSpec((1, 1), lambda b, h, i: (0, 0)),
            pl.BlockSpec((DIFF_VD, 1), lambda b, h, i: (0, 0)),
        ],
        out_specs=pl.BlockSpec((tq, DIFF_VD), lambda b, h, i: (b * nq + i, h)),
        out_shape=jax.ShapeDtypeStruct((n, D_MODEL), BF16),
        scratch_shapes=[
            pltpu.VMEM((n_kblocks, DIFF_VD, BLOCK), BF16),
            pltpu.VMEM((2 * tq, hw), BF16),
            pltpu.VMEM((BLOCK, 2 * tq), F32),
            pltpu.VMEM((1, 2 * tq), F32),
            pltpu.VMEM((1, 2 * tq), F32),
            pltpu.VMEM((DIFF_VD, 2 * tq), F32),
        ],
        compiler_params=_params("parallel", "parallel", "arbitrary"),
        name="diffattn",
    )(proj, proj, proj, lam_vecs, lam_init, subln_w)


def _merge_kernel(ret_ref, diff_ref, cb_ref, cc_ref, cx_ref, cch_ref, cxh_ref, g_ref, x_ref,
                  cw_ref, wb_ref, wo_ref, o_ref, *, tm, p_rows):
    start = pl.program_id(0) * tm
    rowi = lax.broadcasted_iota(jnp.int32, (tm, D_MODEL), 0)

    u = cc_ref[...].astype(F32) * cx_ref[...].astype(F32)
    uh = cch_ref[...].astype(F32) * cxh_ref[...].astype(F32)
    h1 = uh[BF16_SUBLANES - 1:BF16_SUBLANES, :]
    h2 = uh[BF16_SUBLANES - 2:BF16_SUBLANES - 1, :]
    u1 = jnp.where(rowi == 0, h1, pltpu.roll(u, 1, axis=0))
    u2 = jnp.where(rowi == 0, h2, jnp.where(rowi == 1, h1, pltpu.roll(u, 2, axis=0)))
    cw = cw_ref[...]
    conv = cb_ref[...].astype(F32) * (cw[0:1] * u2 + cw[1:2] * u1 + cw[2:3] * u)

    g = g_ref[...].astype(F32)
    g = 1.0 / (1.0 + jnp.exp(-g))
    merged = (g[:, 0:D_MODEL] * jnp.dot(ret_ref[...], wb_ref[0], preferred_element_type=F32)
              + g[:, D_MODEL:2 * D_MODEL] * jnp.dot(diff_ref[...], wb_ref[1], preferred_element_type=F32)
              + g[:, 2 * D_MODEL:] * jnp.dot(conv.astype(BF16), wb_ref[2], preferred_element_type=F32))
    y = x_ref[...] + jnp.dot(merged.astype(BF16), wo_ref[...], preferred_element_type=F32)
    r = rowi + (start - (start // p_rows) * p_rows)
    is_pad = (r < META_PAD) | ((r >= p_rows) & (r < p_rows + META_PAD))
    o_ref[...] = jnp.where(is_pad, 0.0, y)


def _merge(ret, diff, proj, x, conv_w, w_branch, w_out, p_rows):
    n, d = x.shape
    tm = _largest_divisor(n, (256, 128))
    halo_blocks = tm // BF16_SUBLANES

    def col(c):
        return lambda i: (i, c // d)

    def halo(c):
        return lambda i: (jnp.maximum(i * halo_blocks - 1, 0), c // d)

    return pl.pallas_call(
        functools.partial(_merge_kernel, tm=tm, p_rows=p_rows),
        grid=(n // tm,),
        in_specs=[
            pl.BlockSpec((tm, d), lambda i: (i, 0)),
            pl.BlockSpec((tm, d), lambda i: (i, 0)),
            pl.BlockSpec((tm, d), col(COL_CB)),
            pl.BlockSpec((tm, d), col(COL_CC)),
            pl.BlockSpec((tm, d), col(COL_CX)),
            pl.BlockSpec((BF16_SUBLANES, d), halo(COL_CC)),
            pl.BlockSpec((BF16_SUBLANES, d), halo(COL_CX)),
            pl.BlockSpec((tm, N_BRANCH * d), lambda i: (i, COL_GATE // (N_BRANCH * d))),
            pl.BlockSpec((tm, d), lambda i: (i, 0)),
            pl.BlockSpec((CONV_K, d), lambda i: (0, 0)),
            pl.BlockSpec((N_BRANCH, d, d), lambda i: (0, 0, 0)),
            pl.BlockSpec((d, d), lambda i: (0, 0)),
        ],
        out_specs=pl.BlockSpec((tm, d), lambda i: (i, 0)),
        out_shape=jax.ShapeDtypeStruct((n, d), F32),
        compiler_params=_params("parallel"),
        name="merge",
    )(ret, diff, proj, proj, proj, proj, proj, proj, x, conv_w, w_branch, w_out)


def _mlp_kernel(x_ref, nw_ref, wu_ref, wd_ref, o_ref, *, ff_chunk):
    x = x_ref[...]
    xn = _rms(x, nw_ref[...]).astype(BF16)
    acc = x
    for c in range(wu_ref.shape[1] // ff_chunk):
        cols = slice(c * ff_chunk, (c + 1) * ff_chunk)
        u = jnp.maximum(jnp.dot(xn, wu_ref[:, cols], preferred_element_type=F32), 0.0)
        acc = acc + jnp.dot((u * u).astype(BF16), wd_ref[cols, :], preferred_element_type=F32)
    o_ref[...] = acc


def _mlp(x, norm_w, w_up, w_down):
    n, d = x.shape
    d_ff = w_up.shape[1]
    tm = _largest_divisor(n, (512, 256, 128))
    return pl.pallas_call(
        functools.partial(_mlp_kernel, ff_chunk=1024),
        grid=(n // tm,),
        in_specs=[
            pl.BlockSpec((tm, d), lambda i: (i, 0)),
            pl.BlockSpec((1, d), lambda i: (0, 0)),
            pl.BlockSpec((d, d_ff), lambda i: (0, 0)),
            pl.BlockSpec((d_ff, d), lambda i: (0, 0)),
        ],
        out_specs=pl.BlockSpec((tm, d), lambda i: (i, 0)),
        out_shape=jax.ShapeDtypeStruct((n, d), F32),
        compiler_params=_params("parallel"),
        name="mlp",
    )(x, norm_w, w_up, w_down)


def _final_norm_kernel(x_ref, w_ref, o_ref):
    o_ref[0] = _rms(x_ref[0], w_ref[...])


def _final_norm(h3, w, seq):
    batch, _, d = h3.shape
    return pl.pallas_call(
        _final_norm_kernel,
        grid=(batch, seq // BLOCK),
        in_specs=[
            pl.BlockSpec((1, BLOCK, d), lambda b, i: (b, i + 1, 0)),
            pl.BlockSpec((1, d), lambda b, i: (0, 0)),
        ],
        out_specs=pl.BlockSpec((1, BLOCK, d), lambda b, i: (b, i, 0)),
        out_shape=jax.ShapeDtypeStruct((batch, seq, d), h3.dtype),
        compiler_params=_params("parallel", "parallel"),
        name="final_norm",
    )(h3, w)


def kernel(x, meta_tokens, norm1_w, w_in, conv_w, diff_lambda, diff_subln_w, w_branch, w_out,
           norm2_w, w_up, w_down, final_norm_w):
    batch, seq, d = x.shape
    depth = w_in.shape[0]
    assert d == D_MODEL and w_in.shape[1:] == (D_MODEL, D_IN) and seq % BLOCK == 0
    p_rows = META_PAD + N_META + seq

    meta = jnp.broadcast_to(meta_tokens.astype(x.dtype)[None], (batch, N_META, d))
    h = jnp.concatenate([jnp.zeros((batch, META_PAD, d), x.dtype), meta, x], axis=1).reshape(batch * p_rows, d)

    layer_ids = jnp.arange(depth, dtype=F32)
    lam_inits = (0.8 - 0.6 * jnp.exp(-0.3 * layer_ids)).reshape(depth, 1, 1)
    layers = dict(
        norm1_w=norm1_w.reshape(depth, 1, d), w_in=w_in.astype(BF16), conv_w=conv_w,
        lam_vecs=diff_lambda.astype(F32), lam_init=lam_inits, subln_w=diff_subln_w.reshape(depth, DIFF_VD, 1),
        w_branch=w_branch.astype(BF16), w_out=w_out.astype(BF16), norm2_w=norm2_w.reshape(depth, 1, d),
        w_up=w_up.astype(BF16), w_down=w_down.astype(BF16))

    def layer(h, p):
        proj = _inproj(h, p["norm1_w"], p["w_in"])
        ret = _retention(proj, batch, p_rows)
        diff = _diff_attention(proj, p["lam_vecs"], p["lam_init"], p["subln_w"], batch, p_rows)
        h = _merge(ret, diff, proj, h, p["conv_w"], p["w_branch"], p["w_out"], p_rows)
        h = _mlp(h, p["norm2_w"], p["w_up"], p["w_down"])
        return h, None

    h, _ = lax.scan(layer, h, layers)
    return _final_norm(h.reshape(batch, p_rows, d), final_norm_w.reshape(1, d), seq)
```

```python
import functools
import math

import jax
import jax.numpy as jnp
from jax import lax
from jax.experimental import pallas as pl
from jax.experimental.pallas import tpu as pltpu

N_META = 16
BLOCK = 128
META_PAD = BLOCK - N_META
RET_HEADS = 4
RET_DK = 128
RET_DV = 256
DIFF_HEADS = 8
DIFF_HD = 64
DIFF_VD = 128
CONV_K = 3
N_BRANCH = 3
EPS = 1e-6
NEG_INF = -1e30

D_MODEL = RET_HEADS * RET_DV
COL_RQ, COL_RK, COL_RV, COL_RG = 0, 512, 1024, 2048
COL_DQ, COL_DK, COL_DV = 3072, 4096, 5120
COL_CB, COL_CC, COL_CX, COL_GATE = 6144, 7168, 8192, 9216
D_IN = COL_GATE + N_BRANCH * D_MODEL

V7X_VMEM_LIMIT_BYTES = 56 * 1024 * 1024
BF16_SUBLANES = 16
ATTN_KEY_TILE = 512
ATTN_HEADS_PER_STEP = 4

F32 = jnp.float32
BF16 = jnp.bfloat16


def _largest_divisor(n, candidates):
    for c in candidates:
        if n % c == 0:
            return c
    raise ValueError(f"no tile in {candidates} divides {n}")


def _params(*semantics):
    return pltpu.CompilerParams(dimension_semantics=semantics, vmem_limit_bytes=V7X_VMEM_LIMIT_BYTES)


def _rms(x, w):
    return x * lax.rsqrt(jnp.mean(x * x, axis=-1, keepdims=True) + EPS) * w


def _inproj_kernel(x_ref, nw_ref, w_ref, o_ref, xn_ref):
    @pl.when(pl.program_id(1) == 0)
    def _():
        xn_ref[...] = _rms(x_ref[...], nw_ref[...]).astype(BF16)

    o_ref[...] = jnp.dot(xn_ref[...], w_ref[...], preferred_element_type=F32).astype(o_ref.dtype)


def _inproj(h, norm_w, w_in):
    n, d = h.shape
    tm = _largest_divisor(n, (1024, 512, 256, 128))
    tn = 2048
    return pl.pallas_call(
        _inproj_kernel,
        grid=(n // tm, D_IN // tn),
        in_specs=[
            pl.BlockSpec((tm, d), lambda i, j: (i, 0)),
            pl.BlockSpec((1, d), lambda i, j: (0, 0)),
            pl.BlockSpec((d, tn), lambda i, j: (0, j)),
        ],
        out_specs=pl.BlockSpec((tm, tn), lambda i, j: (i, j)),
        out_shape=jax.ShapeDtypeStruct((n, D_IN), BF16),
        scratch_shapes=[pltpu.VMEM((tm, d), BF16)],
        compiler_params=_params("parallel", "arbitrary"),
        name="inproj",
    )(h, norm_w, w_in)


def _ret_log_gamma(head):
    return math.log1p(-(2.0 ** (-5.0 - head)))


def _retention_kernel(q_ref, k_ref, v_ref, g_ref, o_ref, state_ref, intra_ref, qd_ref, kd_ref, *, chunks):
    @pl.when(pl.program_id(1) == 0)
    def _():
        state_ref[...] = jnp.zeros_like(state_ref)
        scale = RET_DK ** -0.5
        i_sq = lax.broadcasted_iota(jnp.int32, (BLOCK, BLOCK), 0).astype(F32)
        j_sq = lax.broadcasted_iota(jnp.int32, (BLOCK, BLOCK), 1).astype(F32)
        i_dv = lax.broadcasted_iota(jnp.int32, (BLOCK, RET_DV), 0).astype(F32)
        dist = i_sq - j_sq
        for h in range(RET_HEADS):
            lg = _ret_log_gamma(h)
            intra_ref[h] = jnp.where(dist >= 0, jnp.exp(lg * jnp.maximum(dist, 0.0)), 0.0) * scale
            qd_ref[h] = jnp.exp(lg * (i_dv + 1.0))
            kd_ref[h] = jnp.exp(lg * (BLOCK - 1.0 - i_sq)) * scale

    for c in range(chunks):
        rows = pl.ds(c * BLOCK, BLOCK)
        for h in range(RET_HEADS):
            s_decay = math.exp(_ret_log_gamma(h) * BLOCK)
            q = q_ref[rows, h * RET_DK:(h + 1) * RET_DK]
            k = k_ref[rows, h * RET_DK:(h + 1) * RET_DK]
            v = v_ref[rows, h * RET_DV:(h + 1) * RET_DV]
            st = state_ref[h]
            scores = lax.dot_general(q, k, (((1,), (1,)), ((), ())), preferred_element_type=F32)
            scores = (scores * intra_ref[h]).astype(BF16)
            out = (jnp.dot(scores, v, preferred_element_type=F32)
                   + qd_ref[h] * jnp.dot(q, st.astype(BF16), preferred_element_type=F32))
            kdec = (k.astype(F32) * kd_ref[h]).astype(BF16)
            state_ref[h] = st * s_decay + lax.dot_general(
                kdec, v, (((0,), (0,)), ((), ())), preferred_element_type=F32)
            mu = jnp.mean(out, axis=-1, keepdims=True)
            cen = out - mu
            var = jnp.mean(cen * cen, axis=-1, keepdims=True)
            g = g_ref[rows, h * RET_DV:(h + 1) * RET_DV].astype(F32)
            silu = g / (1.0 + jnp.exp(-g))
            o_ref[rows, h * RET_DV:(h + 1) * RET_DV] = (cen * lax.rsqrt(var + EPS) * silu).astype(o_ref.dtype)


def _retention(proj, batch, p_rows):
    n = proj.shape[0]
    nc = p_rows // BLOCK
    chunks = _largest_divisor(nc, (5, 3, 1))
    steps = nc // chunks
    tr = chunks * BLOCK
    qk_w = RET_HEADS * RET_DK

    def rows(b, c):
        return b * steps + c

    return pl.pallas_call(
        functools.partial(_retention_kernel, chunks=chunks),
        grid=(batch, steps),
        in_specs=[
            pl.BlockSpec((tr, qk_w), lambda b, c: (rows(b, c), COL_RQ // qk_w)),
            pl.BlockSpec((tr, qk_w), lambda b, c: (rows(b, c), COL_RK // qk_w)),
            pl.BlockSpec((tr, D_MODEL), lambda b, c: (rows(b, c), COL_RV // D_MODEL)),
            pl.BlockSpec((tr, D_MODEL), lambda b, c: (rows(b, c), COL_RG // D_MODEL)),
        ],
        out_specs=pl.BlockSpec((tr, D_MODEL), lambda b, c: (rows(b, c), 0)),
        out_shape=jax.ShapeDtypeStruct((n, D_MODEL), BF16),
        scratch_shapes=[
            pltpu.VMEM((RET_HEADS, RET_DK, RET_DV), F32),
            pltpu.VMEM((RET_HEADS, BLOCK, BLOCK), F32),
            pltpu.VMEM((RET_HEADS, BLOCK, RET_DV), F32),
            pltpu.VMEM((RET_HEADS, BLOCK, BLOCK), F32),
        ],
        compiler_params=_params("parallel", "arbitrary"),
        name="retention",
    )(proj, proj, proj, proj)


FEAT_BLK, FEAT_ROW, FEAT_MASK, FEAT_Q0, FEAT_QROW = 0, 3, 6, 7, 10
FEAT_SPLIT = 3
LOG2E = math.log2(math.e)
VT_ROWS = DIFF_VD + 16


def _kvprep_kernel(k_ref, v_ref, ka_ref, vt_ref, *, tk, n_real, n_tiles):
    blocks_per_tile = tk // BLOCK
    hw = 2 * DIFF_HD
    lane = lax.broadcasted_iota(jnp.int32, (BLOCK, hw), 1)
    row = lax.broadcasted_iota(jnp.int32, (BLOCK, hw), 0)
    ones_row = jnp.where(lax.broadcasted_iota(jnp.int32, (VT_ROWS - DIFF_VD, BLOCK), 0) == 0, 1.0, 0.0)
    for t in range(n_tiles):
        for bi in range(blocks_per_tile):
            blk = t * blocks_per_tile + bi
            rows = pl.ds(bi * BLOCK, BLOCK)
            cols = pl.ds(bi * BLOCK, BLOCK)
            real = blk < n_real
            masked = (row + blk * BLOCK < META_PAD) if real else (row >= 0)
            feat = jnp.where(lane < FEAT_ROW, float(blk),
                   jnp.where(lane < FEAT_MASK, row.astype(F32),
                   jnp.where(lane == FEAT_MASK, jnp.where(masked, 1.0, 0.0),
                   jnp.where(lane < FEAT_QROW + FEAT_SPLIT, 1.0, 0.0))))
            ka_ref[0, t, rows, hw:] = feat.astype(BF16)
            vt_ref[0, t, DIFF_VD:, cols] = ones_row.astype(BF16)
            if real:
                ka_ref[0, t, rows, 0:hw] = k_ref[blk * BLOCK:(blk + 1) * BLOCK, :]
                vt_ref[0, t, 0:DIFF_VD, cols] = v_ref[blk * BLOCK:(blk + 1) * BLOCK, :].astype(F32).T.astype(BF16)
            else:
                ka_ref[0, t, rows, 0:hw] = jnp.zeros((BLOCK, hw), BF16)
                vt_ref[0, t, 0:DIFF_VD, cols] = jnp.zeros((DIFF_VD, BLOCK), BF16)


def _kvprep(proj, batch, p_rows, tk):
    n_real = p_rows // BLOCK
    n_tiles = -(-p_rows // tk)
    hw = 2 * DIFF_HD
    return pl.pallas_call(
        functools.partial(_kvprep_kernel, tk=tk, n_real=n_real, n_tiles=n_tiles),
        grid=(batch, DIFF_HEADS),
        in_specs=[
            pl.BlockSpec((p_rows, hw), lambda b, h: (b, COL_DK // hw + h)),
            pl.BlockSpec((p_rows, DIFF_VD), lambda b, h: (b, COL_DV // DIFF_VD + h)),
        ],
        out_specs=[
            pl.BlockSpec((1, n_tiles, tk, 2 * hw), lambda b, h: (b * DIFF_HEADS + h, 0, 0, 0)),
            pl.BlockSpec((1, n_tiles, VT_ROWS, tk), lambda b, h: (b * DIFF_HEADS + h, 0, 0, 0)),
        ],
        out_shape=[
            jax.ShapeDtypeStruct((batch * DIFF_HEADS, n_tiles, tk, 2 * hw), BF16),
            jax.ShapeDtypeStruct((batch * DIFF_HEADS, n_tiles, VT_ROWS, tk), BF16),
        ],
        compiler_params=_params("parallel", "parallel"),
        name="kvprep",
    )(proj, proj)


def _attn_kernel(q_ref, k_ref, vt_ref, lamv_ref, lami_ref, sw_ref, o_ref,
                 s_ref, smax_ref, m_ref, acc_ref, *, tq, tk, heads):
    hp = pl.program_id(1)
    qi = pl.program_id(2)
    q0 = qi * tq
    lanes = 2 * tq
    n_full = q0 // tk
    hw = 2 * DIFF_HD

    lane_h = lax.broadcasted_iota(jnp.int32, (tq, hw), 1)
    row_q = lax.broadcasted_iota(jnp.int32, (tq, hw), 0).astype(F32)
    lv = lamv_ref[...]
    lam = (jnp.exp(jnp.sum(lv[0:1] * lv[1:2], axis=-1, keepdims=True))
           - jnp.exp(jnp.sum(lv[2:3] * lv[3:4], axis=-1, keepdims=True)) + lami_ref[...])

    def split_into(feat, first_lane, x):
        for piece in range(FEAT_SPLIT):
            head_part = x.astype(BF16).astype(F32)
            feat = jnp.where(lane_h == first_lane + piece, head_part, feat)
            x = x - head_part
        return feat

    def qcat_of(hh):
        head = (hp * heads + hh).astype(F32)
        slope2 = jnp.exp2(-(jnp.zeros((tq, hw), F32) + head + 1.0)) * LOG2E
        feat = jnp.where(lane_h == FEAT_MASK, NEG_INF, 0.0)
        feat = split_into(feat, FEAT_BLK, slope2 * BLOCK)
        feat = split_into(feat, FEAT_ROW, slope2)
        feat = split_into(feat, FEAT_Q0, -slope2 * q0.astype(F32))
        feat = split_into(feat, FEAT_QROW, -slope2 * row_q)
        feat = feat.astype(BF16)
        q = q_ref[:, hh * hw:(hh + 1) * hw]
        q_m0 = jnp.where(lane_h < DIFF_HD, q, jnp.zeros_like(q))
        q_m1 = jnp.where(lane_h >= DIFF_HD, q, jnp.zeros_like(q))
        return jnp.concatenate([jnp.concatenate([q_m0, feat], axis=1),
                                jnp.concatenate([q_m1, feat], axis=1)], axis=0)

    qcats = [qcat_of(hh) for hh in range(heads)]

    def scores(j, slot):
        for hh in range(heads):
            s = lax.dot_general(k_ref[hh, j], qcats[hh], (((1,), (1,)), ((), ())),
                                preferred_element_type=F32)
            s_ref[slot, hh] = s
            smax_ref[slot, hh] = jnp.max(s, axis=0, keepdims=True)

    def update(j, slot, masked):
        for hh in range(heads):
            s = s_ref[slot, hh]
            s_max = smax_ref[slot, hh]
            if masked:
                kpos = lax.broadcasted_iota(jnp.int32, (tk, lanes), 0) + j * tk
                qpos = lax.broadcasted_iota(jnp.int32, (tk, lanes), 1)
                qpos = jnp.where(qpos >= tq, qpos - tq, qpos) + q0
                s = jnp.where(kpos <= qpos, s, NEG_INF)
                s_max = jnp.max(s, axis=0, keepdims=True)
            m_old = m_ref[hh]
            m_new = jnp.maximum(m_old, s_max)
            alpha = jnp.exp2(m_old - m_new)
            p = jnp.exp2(s - m_new).astype(BF16)
            m_ref[hh] = m_new
            pv = jnp.dot(vt_ref[hh, j], p, preferred_element_type=F32)
            acc_ref[hh] = acc_ref[hh] * alpha + pv

    m_ref[...] = jnp.full_like(m_ref, NEG_INF)
    acc_ref[...] = jnp.zeros_like(acc_ref)

    scores(0, 0)

    def body2(i, carry):
        j = 2 * i
        scores(j + 1, 1)
        update(j, 0, False)
        scores(j + 2, 0)
        update(j + 1, 1, False)
        return carry

    def body1(j, carry):
        scores(j + 1, 1)
        update(j, 0, False)
        return carry

    pairs = n_full // 2
    lax.fori_loop(0, pairs, body2, 0)
    lax.fori_loop(2 * pairs, n_full, body1, 0)
    update(n_full, n_full - 2 * pairs, True)

    for hh in range(heads):
        o = acc_ref[hh, 0:DIFF_VD] / acc_ref[hh, DIFF_VD:DIFF_VD + 1]
        dlt = o[:, 0:tq] - lam * o[:, tq:lanes]
        ms = jnp.mean(dlt * dlt, axis=0, keepdims=True)
        y = dlt * lax.rsqrt(ms + EPS) * sw_ref[...] * (1.0 - lami_ref[...])
        o_ref[:, hh * DIFF_VD:(hh + 1) * DIFF_VD] = y.T.astype(o_ref.dtype)


def _diff_attention(proj, lam_vecs, lam_init, subln_w, batch, p_rows):
    n = proj.shape[0]
    tq = BLOCK
    tk = ATTN_KEY_TILE
    heads = ATTN_HEADS_PER_STEP
    nq = p_rows // tq
    hw = 2 * DIFF_HD
    k_aug, v_t = _kvprep(proj, batch, p_rows, tk)
    n_tiles = k_aug.shape[1]
    groups = DIFF_HEADS // heads
    return pl.pallas_call(
        functools.partial(_attn_kernel, tq=tq, tk=tk, heads=heads),
        grid=(batch, groups, nq),
        in_specs=[
            pl.BlockSpec((tq, heads * hw), lambda b, g, i: (b * nq + i, COL_DQ // (heads * hw) + g)),
            pl.BlockSpec((heads, n_tiles, tk, 2 * hw), lambda b, g, i: (b * groups + g, 0, 0, 0),
                         pipeline_mode=pl.Buffered(1)),
            pl.BlockSpec((heads, n_tiles, VT_ROWS, tk), lambda b, g, i: (b * groups + g, 0, 0, 0),
                         pipeline_mode=pl.Buffered(1)),
            pl.BlockSpec((4, DIFF_HD), lambda b, g, i: (0, 0)),
            pl.BlockSpec((1, 1), lambda b, g, i: (0, 0)),
            pl.BlockSpec((DIFF_VD, 1), lambda b, g, i: (0, 0)),
        ],
        out_specs=pl.BlockSpec((tq, heads * DIFF_VD), lambda b, g, i: (b * nq + i, g)),
        out_shape=jax.ShapeDtypeStruct((n, D_MODEL), BF16),
        scratch_shapes=[
            pltpu.VMEM((2, heads, tk, 2 * tq), F32),
            pltpu.VMEM((2, heads, 1, 2 * tq), F32),
            pltpu.VMEM((heads, 1, 2 * tq), F32),
            pltpu.VMEM((heads, VT_ROWS, 2 * tq), F32),
        ],
        compiler_params=_params("parallel", "parallel", "arbitrary"),
        name="diffattn",
    )(proj, k_aug, v_t, lam_vecs, lam_init, subln_w)


def _merge_kernel(ret_ref, diff_ref, cb_ref, cc_ref, cx_ref, cch_ref, cxh_ref, g_ref, x_ref,
                  cw_ref, wb_ref, wo_ref, o_ref, *, tm, p_rows):
    start = pl.program_id(0) * tm
    rowi = lax.broadcasted_iota(jnp.int32, (tm, D_MODEL), 0)

    u = cc_ref[...].astype(F32) * cx_ref[...].astype(F32)
    uh = cch_ref[...].astype(F32) * cxh_ref[...].astype(F32)
    h1 = uh[BF16_SUBLANES - 1:BF16_SUBLANES, :]
    h2 = uh[BF16_SUBLANES - 2:BF16_SUBLANES - 1, :]
    u1 = jnp.where(rowi == 0, h1, pltpu.roll(u, 1, axis=0))
    u2 = jnp.where(rowi == 0, h2, jnp.where(rowi == 1, h1, pltpu.roll(u, 2, axis=0)))
    cw = cw_ref[...]
    conv = cb_ref[...].astype(F32) * (cw[0:1] * u2 + cw[1:2] * u1 + cw[2:3] * u)

    g = g_ref[...].astype(F32)
    g = 1.0 / (1.0 + jnp.exp(-g))
    merged = (g[:, 0:D_MODEL] * jnp.dot(ret_ref[...], wb_ref[0], preferred_element_type=F32)
              + g[:, D_MODEL:2 * D_MODEL] * jnp.dot(diff_ref[...], wb_ref[1], preferred_element_type=F32)
              + g[:, 2 * D_MODEL:] * jnp.dot(conv.astype(BF16), wb_ref[2], preferred_element_type=F32))
    y = x_ref[...] + jnp.dot(merged.astype(BF16), wo_ref[...], preferred_element_type=F32)
    r = rowi + (start - (start // p_rows) * p_rows)
    is_pad = (r < META_PAD) | ((r >= p_rows) & (r < p_rows + META_PAD))
    o_ref[...] = jnp.where(is_pad, 0.0, y)


def _merge(ret, diff, proj, x, conv_w, w_branch, w_out, p_rows):
    n, d = x.shape
    tm = _largest_divisor(n, (256, 128))
    halo_blocks = tm // BF16_SUBLANES

    def col(c):
        return lambda i: (i, c // d)

    def halo(c):
        return lambda i: (jnp.maximum(i * halo_blocks - 1, 0), c // d)

    return pl.pallas_call(
        functools.partial(_merge_kernel, tm=tm, p_rows=p_rows),
        grid=(n // tm,),
        in_specs=[
            pl.BlockSpec((tm, d), lambda i: (i, 0)),
            pl.BlockSpec((tm, d), lambda i: (i, 0)),
            pl.BlockSpec((tm, d), col(COL_CB)),
            pl.BlockSpec((tm, d), col(COL_CC)),
            pl.BlockSpec((tm, d), col(COL_CX)),
            pl.BlockSpec((BF16_SUBLANES, d), halo(COL_CC)),
            pl.BlockSpec((BF16_SUBLANES, d), halo(COL_CX)),
            pl.BlockSpec((tm, N_BRANCH * d), lambda i: (i, COL_GATE // (N_BRANCH * d))),
            pl.BlockSpec((tm, d), lambda i: (i, 0)),
            pl.BlockSpec((CONV_K, d), lambda i: (0, 0)),
            pl.BlockSpec((N_BRANCH, d, d), lambda i: (0, 0, 0)),
            pl.BlockSpec((d, d), lambda i: (0, 0)),
        ],
        out_specs=pl.BlockSpec((tm, d), lambda i: (i, 0)),
        out_shape=jax.ShapeDtypeStruct((n, d), F32),
        compiler_params=_params("parallel"),
        name="merge",
    )(ret, diff, proj, proj, proj, proj, proj, proj, x, conv_w, w_branch, w_out)


def _mlp_kernel(x_ref, nw_ref, wu_ref, wd_ref, o_ref, *, ff_chunk):
    x = x_ref[...]
    xn = _rms(x, nw_ref[...]).astype(BF16)
    acc = x
    for c in range(wu_ref.shape[1] // ff_chunk):
        cols = slice(c * ff_chunk, (c + 1) * ff_chunk)
        u = jnp.maximum(jnp.dot(xn, wu_ref[:, cols], preferred_element_type=F32), 0.0)
        acc = acc + jnp.dot((u * u).astype(BF16), wd_ref[cols, :], preferred_element_type=F32)
    o_ref[...] = acc


def _mlp(x, norm_w, w_up, w_down):
    n, d = x.shape
    d_ff = w_up.shape[1]
    tm = _largest_divisor(n, (512, 256, 128))
    return pl.pallas_call(
        functools.partial(_mlp_kernel, ff_chunk=1024),
        grid=(n // tm,),
        in_specs=[
            pl.BlockSpec((tm, d), lambda i: (i, 0)),
            pl.BlockSpec((1, d), lambda i: (0, 0)),
            pl.BlockSpec((d, d_ff), lambda i: (0, 0)),
            pl.BlockSpec((d_ff, d), lambda i: (0, 0)),
        ],
        out_specs=pl.BlockSpec((tm, d), lambda i: (i, 0)),
        out_shape=jax.ShapeDtypeStruct((n, d), F32),
        compiler_params=_params("parallel"),
        name="mlp",
    )(x, norm_w, w_up, w_down)


def _final_norm_kernel(x_ref, w_ref, o_ref):
    o_ref[0] = _rms(x_ref[0], w_ref[...])


def _final_norm(h3, w, seq):
    batch, _, d = h3.shape
    return pl.pallas_call(
        _final_norm_kernel,
        grid=(batch, seq // BLOCK),
        in_specs=[
            pl.BlockSpec((1, BLOCK, d), lambda b, i: (b, i + 1, 0)),
            pl.BlockSpec((1, d), lambda b, i: (0, 0)),
        ],
        out_specs=pl.BlockSpec((1, BLOCK, d), lambda b, i: (b, i, 0)),
        out_shape=jax.ShapeDtypeStruct((batch, seq, d), h3.dtype),
        compiler_params=_params("parallel", "parallel"),
        name="final_norm",
    )(h3, w)


def kernel(x, meta_tokens, norm1_w, w_in, conv_w, diff_lambda, diff_subln_w, w_branch, w_out,
           norm2_w, w_up, w_down, final_norm_w):
    batch, seq, d = x.shape
    depth = w_in.shape[0]
    assert d == D_MODEL and w_in.shape[1:] == (D_MODEL, D_IN) and seq % BLOCK == 0
    p_rows = META_PAD + N_META + seq

    meta = jnp.broadcast_to(meta_tokens.astype(x.dtype)[None], (batch, N_META, d))
    h = jnp.concatenate([jnp.zeros((batch, META_PAD, d), x.dtype), meta, x], axis=1).reshape(batch * p_rows, d)

    layer_ids = jnp.arange(depth, dtype=F32)
    lam_inits = (0.8 - 0.6 * jnp.exp(-0.3 * layer_ids)).reshape(depth, 1, 1)
    w_in = w_in.at[:, :, COL_DQ:COL_DK].multiply(DIFF_HD ** -0.5 * LOG2E)
    layers = dict(
        norm1_w=norm1_w.reshape(depth, 1, d), w_in=w_in.astype(BF16), conv_w=conv_w,
        lam_vecs=diff_lambda.astype(F32), lam_init=lam_inits, subln_w=diff_subln_w.reshape(depth, DIFF_VD, 1),
        w_branch=w_branch.astype(BF16), w_out=w_out.astype(BF16), norm2_w=norm2_w.reshape(depth, 1, d),
        w_up=w_up.astype(BF16), w_down=w_down.astype(BF16))

    def layer(h, p):
        proj = _inproj(h, p["norm1_w"], p["w_in"])
        ret = _retention(proj, batch, p_rows)
        diff = _diff_attention(proj, p["lam_vecs"], p["lam_init"], p["subln_w"], batch, p_rows)
        h = _merge(ret, diff, proj, h, p["conv_w"], p["w_branch"], p["w_out"], p_rows)
        h = _mlp(h, p["norm2_w"], p["w_up"], p["w_down"])
        return h, None

    h, _ = lax.scan(layer, h, layers)
    return _final_norm(h.reshape(batch, p_rows, d), final_norm_w.reshape(1, d), seq)
```

```python
import functools
import math

import jax
import jax.numpy as jnp
from jax import lax
from jax.experimental import pallas as pl
from jax.experimental.pallas import tpu as pltpu

N_META = 16
BLOCK = 128
META_PAD = BLOCK - N_META
RET_HEADS = 4
RET_DK = 128
RET_DV = 256
DIFF_HEADS = 8
DIFF_HD = 64
DIFF_VD = 128
CONV_K = 3
N_BRANCH = 3
EPS = 1e-6
NEG_INF = -1e30

D_MODEL = RET_HEADS * RET_DV
COL_RQ, COL_RK, COL_RV, COL_RG = 0, 512, 1024, 2048
COL_DQ, COL_DK, COL_DV = 3072, 4096, 5120
COL_CB, COL_CC, COL_CX, COL_GATE = 6144, 7168, 8192, 9216
D_IN = COL_GATE + N_BRANCH * D_MODEL

V7X_VMEM_LIMIT_BYTES = 56 * 1024 * 1024
BF16_SUBLANES = 16
ATTN_KEY_TILE = 512
ATTN_HEADS_PER_STEP = 4

F32 = jnp.float32
BF16 = jnp.bfloat16


def _largest_divisor(n, candidates):
    for c in candidates:
        if n % c == 0:
            return c
    raise ValueError(f"no tile in {candidates} divides {n}")


def _params(*semantics):
    return pltpu.CompilerParams(dimension_semantics=semantics, vmem_limit_bytes=V7X_VMEM_LIMIT_BYTES)


def _rms(x, w):
    return x * lax.rsqrt(jnp.mean(x * x, axis=-1, keepdims=True) + EPS) * w


def _inproj_kernel(x_ref, nw_ref, w_ref, o_ref, xn_ref):
    @pl.when(pl.program_id(1) == 0)
    def _():
        xn_ref[...] = _rms(x_ref[...], nw_ref[...]).astype(BF16)

    o_ref[...] = jnp.dot(xn_ref[...], w_ref[...], preferred_element_type=F32).astype(o_ref.dtype)


def _inproj(h, norm_w, w_in):
    n, d = h.shape
    tm = _largest_divisor(n, (1024, 512, 256, 128))
    tn = 2048
    return pl.pallas_call(
        _inproj_kernel,
        grid=(n // tm, D_IN // tn),
        in_specs=[
            pl.BlockSpec((tm, d), lambda i, j: (i, 0)),
            pl.BlockSpec((1, d), lambda i, j: (0, 0)),
            pl.BlockSpec((d, tn), lambda i, j: (0, j)),
        ],
        out_specs=pl.BlockSpec((tm, tn), lambda i, j: (i, j)),
        out_shape=jax.ShapeDtypeStruct((n, D_IN), BF16),
        scratch_shapes=[pltpu.VMEM((tm, d), BF16)],
        compiler_params=_params("parallel", "arbitrary"),
        name="inproj",
    )(h, norm_w, w_in)


def _ret_log_gamma(head):
    return math.log1p(-(2.0 ** (-5.0 - head)))


def _retention_kernel(q_ref, k_ref, v_ref, g_ref, o_ref, state_ref, intra_ref, qd_ref, kd_ref, *, chunks):
    @pl.when(pl.program_id(1) == 0)
    def _():
        state_ref[...] = jnp.zeros_like(state_ref)
        scale = RET_DK ** -0.5
        i_sq = lax.broadcasted_iota(jnp.int32, (BLOCK, BLOCK), 0).astype(F32)
        j_sq = lax.broadcasted_iota(jnp.int32, (BLOCK, BLOCK), 1).astype(F32)
        i_dv = lax.broadcasted_iota(jnp.int32, (BLOCK, RET_DV), 0).astype(F32)
        dist = i_sq - j_sq
        for h in range(RET_HEADS):
            lg = _ret_log_gamma(h)
            intra_ref[h] = jnp.where(dist >= 0, jnp.exp(lg * jnp.maximum(dist, 0.0)), 0.0) * scale
            qd_ref[h] = jnp.exp(lg * (i_dv + 1.0))
            kd_ref[h] = jnp.exp(lg * (BLOCK - 1.0 - i_sq)) * scale

    for c in range(chunks):
        rows = pl.ds(c * BLOCK, BLOCK)
        for h in range(RET_HEADS):
            s_decay = math.exp(_ret_log_gamma(h) * BLOCK)
            q = q_ref[rows, h * RET_DK:(h + 1) * RET_DK]
            k = k_ref[rows, h * RET_DK:(h + 1) * RET_DK]
            v = v_ref[rows, h * RET_DV:(h + 1) * RET_DV]
            st = state_ref[h]
            scores = lax.dot_general(q, k, (((1,), (1,)), ((), ())), preferred_element_type=F32)
            scores = (scores * intra_ref[h]).astype(BF16)
            out = (jnp.dot(scores, v, preferred_element_type=F32)
                   + qd_ref[h] * jnp.dot(q, st.astype(BF16), preferred_element_type=F32))
            kdec = (k.astype(F32) * kd_ref[h]).astype(BF16)
            state_ref[h] = st * s_decay + lax.dot_general(
                kdec, v, (((0,), (0,)), ((), ())), preferred_element_type=F32)
            mu = jnp.mean(out, axis=-1, keepdims=True)
            cen = out - mu
            var = jnp.mean(cen * cen, axis=-1, keepdims=True)
            g = g_ref[rows, h * RET_DV:(h + 1) * RET_DV].astype(F32)
            silu = g / (1.0 + jnp.exp(-g))
            o_ref[rows, h * RET_DV:(h + 1) * RET_DV] = (cen * lax.rsqrt(var + EPS) * silu).astype(o_ref.dtype)


def _retention(proj, batch, p_rows):
    n = proj.shape[0]
    nc = p_rows // BLOCK
    chunks = _largest_divisor(nc, (5, 3, 1))
    steps = nc // chunks
    tr = chunks * BLOCK
    qk_w = RET_HEADS * RET_DK

    def rows(b, c):
        return b * steps + c

    return pl.pallas_call(
        functools.partial(_retention_kernel, chunks=chunks),
        grid=(batch, steps),
        in_specs=[
            pl.BlockSpec((tr, qk_w), lambda b, c: (rows(b, c), COL_RQ // qk_w)),
            pl.BlockSpec((tr, qk_w), lambda b, c: (rows(b, c), COL_RK // qk_w)),
            pl.BlockSpec((tr, D_MODEL), lambda b, c: (rows(b, c), COL_RV // D_MODEL)),
            pl.BlockSpec((tr, D_MODEL), lambda b, c: (rows(b, c), COL_RG // D_MODEL)),
        ],
        out_specs=pl.BlockSpec((tr, D_MODEL), lambda b, c: (rows(b, c), 0)),
        out_shape=jax.ShapeDtypeStruct((n, D_MODEL), BF16),
        scratch_shapes=[
            pltpu.VMEM((RET_HEADS, RET_DK, RET_DV), F32),
            pltpu.VMEM((RET_HEADS, BLOCK, BLOCK), F32),
            pltpu.VMEM((RET_HEADS, BLOCK, RET_DV), F32),
            pltpu.VMEM((RET_HEADS, BLOCK, BLOCK), F32),
        ],
        compiler_params=_params("parallel", "arbitrary"),
        name="retention",
    )(proj, proj, proj, proj)


FEAT_BLK, FEAT_ROW, FEAT_MASK, FEAT_Q0, FEAT_QROW = 0, 3, 6, 7, 10
FEAT_SPLIT = 3
LOG2E = math.log2(math.e)
VT_ROWS = DIFF_VD + 16
SKIP_LOG2_MARGIN = 160.0
SKIP_NORM_SLACK = 1.01


def _max_sq_norm(x, lane):
    sq = x.astype(F32)
    sq = sq * sq
    n0 = jnp.sum(jnp.where(lane < DIFF_HD, sq, 0.0), axis=-1, keepdims=True)
    n1 = jnp.sum(jnp.where(lane >= DIFF_HD, sq, 0.0), axis=-1, keepdims=True)
    return jnp.max(jnp.maximum(n0, n1), axis=0, keepdims=True)


def _first_live_tile(q2, k2, head, *, tk, n_tiles):
    blocks_per_tile = tk // BLOCK
    qi = lax.broadcasted_iota(jnp.int32, (BLOCK, BLOCK), 0)
    tj = lax.broadcasted_iota(jnp.int32, (BLOCK, BLOCK), 1)
    diag_tile = lax.shift_right_logical(qi, blocks_per_tile.bit_length() - 1)
    slope2 = jnp.exp2(-(jnp.zeros((BLOCK, BLOCK), F32) + head + 1.0)) * LOG2E
    kd2 = jnp.max(jnp.where(tj == diag_tile, k2, 0.0), axis=-1, keepdims=True)
    dist_min = (qi * BLOCK - tj * tk - (tk - 1)).astype(F32)
    bound = SKIP_NORM_SLACK * jnp.sqrt(q2) * (jnp.sqrt(k2) + jnp.sqrt(kd2)) + SKIP_LOG2_MARGIN
    dead = jnp.where((slope2 * dist_min > bound) & (tj < diag_tile), 1.0, 0.0)
    run = jnp.ones((BLOCK, 1), F32)
    count = jnp.zeros((BLOCK, 1), F32)
    for j in range(n_tiles):
        run = run * dead[:, j:j + 1]
        count = count + run
    return count


def _kvprep_kernel(q_ref, k_ref, v_ref, ka_ref, vt_ref, js_ref, *, tk, n_real, n_tiles):
    blocks_per_tile = tk // BLOCK
    hw = 2 * DIFF_HD
    lane = lax.broadcasted_iota(jnp.int32, (BLOCK, hw), 1)
    row = lax.broadcasted_iota(jnp.int32, (BLOCK, hw), 0)
    ones_row = jnp.where(lax.broadcasted_iota(jnp.int32, (VT_ROWS - DIFF_VD, BLOCK), 0) == 0, 1.0, 0.0)
    q2 = jnp.zeros((BLOCK, BLOCK), F32)
    k2 = jnp.zeros((BLOCK, BLOCK), F32)
    for blk in range(n_real):
        rows = slice(blk * BLOCK, (blk + 1) * BLOCK)
        q2 = jnp.where(row == blk, _max_sq_norm(q_ref[rows, :], lane), q2)
        k2 = jnp.where(lane == blk // blocks_per_tile, jnp.maximum(k2, _max_sq_norm(k_ref[rows, :], lane)), k2)
    first = _first_live_tile(q2, k2, pl.program_id(1).astype(F32), tk=tk, n_tiles=n_tiles)
    js_ref[0] = jnp.broadcast_to(first, (BLOCK, BLOCK)).astype(jnp.int32)
    for t in range(n_tiles):
        for bi in range(blocks_per_tile):
            blk = t * blocks_per_tile + bi
            rows = pl.ds(bi * BLOCK, BLOCK)
            cols = pl.ds(bi * BLOCK, BLOCK)
            real = blk < n_real
            masked = (row + blk * BLOCK < META_PAD) if real else (row >= 0)
            feat = jnp.where(lane < FEAT_ROW, float(blk),
                   jnp.where(lane < FEAT_MASK, row.astype(F32),
                   jnp.where(lane == FEAT_MASK, jnp.where(masked, 1.0, 0.0),
                   jnp.where(lane < FEAT_QROW + FEAT_SPLIT, 1.0, 0.0))))
            ka_ref[0, t, rows, hw:] = feat.astype(BF16)
            vt_ref[0, t, DIFF_VD:, cols] = ones_row.astype(BF16)
            if real:
                ka_ref[0, t, rows, 0:hw] = k_ref[blk * BLOCK:(blk + 1) * BLOCK, :]
                vt_ref[0, t, 0:DIFF_VD, cols] = v_ref[blk * BLOCK:(blk + 1) * BLOCK, :].astype(F32).T.astype(BF16)
            else:
                ka_ref[0, t, rows, 0:hw] = jnp.zeros((BLOCK, hw), BF16)
                vt_ref[0, t, 0:DIFF_VD, cols] = jnp.zeros((DIFF_VD, BLOCK), BF16)


def _kvprep(proj, batch, p_rows, tk):
    n_real = p_rows // BLOCK
    n_tiles = -(-p_rows // tk)
    hw = 2 * DIFF_HD
    blocks_per_tile = tk // BLOCK
    assert hw == BLOCK and n_real <= BLOCK and blocks_per_tile & (blocks_per_tile - 1) == 0
    return pl.pallas_call(
        functools.partial(_kvprep_kernel, tk=tk, n_real=n_real, n_tiles=n_tiles),
        grid=(batch, DIFF_HEADS),
        in_specs=[
            pl.BlockSpec((p_rows, hw), lambda b, h: (b, COL_DQ // hw + h)),
            pl.BlockSpec((p_rows, hw), lambda b, h: (b, COL_DK // hw + h)),
            pl.BlockSpec((p_rows, DIFF_VD), lambda b, h: (b, COL_DV // DIFF_VD + h)),
        ],
        out_specs=[
            pl.BlockSpec((1, n_tiles, tk, 2 * hw), lambda b, h: (b * DIFF_HEADS + h, 0, 0, 0)),
            pl.BlockSpec((1, n_tiles, VT_ROWS, tk), lambda b, h: (b * DIFF_HEADS + h, 0, 0, 0)),
            pl.BlockSpec((1, BLOCK, BLOCK), lambda b, h: (b * DIFF_HEADS + h, 0, 0)),
        ],
        out_shape=[
            jax.ShapeDtypeStruct((batch * DIFF_HEADS, n_tiles, tk, 2 * hw), BF16),
            jax.ShapeDtypeStruct((batch * DIFF_HEADS, n_tiles, VT_ROWS, tk), BF16),
            jax.ShapeDtypeStruct((batch * DIFF_HEADS, BLOCK, BLOCK), jnp.int32),
        ],
        compiler_params=_params("parallel", "parallel"),
        name="kvprep",
    )(proj, proj, proj)


def _attn_kernel(first_ref, q_ref, k_ref, vt_ref, lamv_ref, lami_ref, sw_ref, o_ref,
                 s_ref, smax_ref, m_ref, acc_ref, *, tq, tk, heads):
    hp = pl.program_id(1)
    qi = pl.program_id(2)
    q0 = qi * tq
    lanes = 2 * tq
    n_full = q0 // tk
    hw = 2 * DIFF_HD
    head0 = pl.program_id(0) * DIFF_HEADS + hp * heads
    j_first = n_full
    for hh in range(heads):
        j_first = jnp.minimum(j_first, first_ref[(head0 + hh) * pl.num_programs(2) + qi])

    lane_h = lax.broadcasted_iota(jnp.int32, (tq, hw), 1)
    row_q = lax.broadcasted_iota(jnp.int32, (tq, hw), 0).astype(F32)
    lv = lamv_ref[...]
    lam = (jnp.exp(jnp.sum(lv[0:1] * lv[1:2], axis=-1, keepdims=True))
           - jnp.exp(jnp.sum(lv[2:3] * lv[3:4], axis=-1, keepdims=True)) + lami_ref[...])

    def split_into(feat, first_lane, x):
        for piece in range(FEAT_SPLIT):
            head_part = x.astype(BF16).astype(F32)
            feat = jnp.where(lane_h == first_lane + piece, head_part, feat)
            x = x - head_part
        return feat

    def qcat_of(hh):
        head = (hp * heads + hh).astype(F32)
        slope2 = jnp.exp2(-(jnp.zeros((tq, hw), F32) + head + 1.0)) * LOG2E
        feat = jnp.where(lane_h == FEAT_MASK, NEG_INF, 0.0)
        feat = split_into(feat, FEAT_BLK, slope2 * BLOCK)
        feat = split_into(feat, FEAT_ROW, slope2)
        feat = split_into(feat, FEAT_Q0, -slope2 * q0.astype(F32))
        feat = split_into(feat, FEAT_QROW, -slope2 * row_q)
        q = q_ref[:, hh * hw:(hh + 1) * hw].astype(F32)
        q_m0 = jnp.where(lane_h < DIFF_HD, q, 0.0)
        q_m1 = jnp.where(lane_h >= DIFF_HD, q, 0.0)
        top = jnp.concatenate([q_m0.T, q_m1.T], axis=1)
        bottom = jnp.concatenate([feat.T, feat.T], axis=1)
        return jnp.concatenate([top, bottom], axis=0).astype(BF16)

    qcats = [qcat_of(hh) for hh in range(heads)]

    def scores(j, slot):
        for hh in range(heads):
            s = jnp.dot(k_ref[hh, j], qcats[hh], preferred_element_type=F32)
            s_ref[slot, hh] = s
            smax_ref[slot, hh] = jnp.max(s, axis=0, keepdims=True)

    def accumulate(hh, s, s_max, vt):
        m_old = m_ref[hh]
        m_new = jnp.maximum(m_old, s_max)
        alpha = jnp.exp2(m_old - m_new)
        p = jnp.exp2(s - m_new).astype(BF16)
        m_ref[hh] = m_new
        pv = jnp.dot(vt, p, preferred_element_type=F32)
        acc_ref[hh] = acc_ref[hh] * alpha + pv

    def update(j, slot):
        for hh in range(heads):
            accumulate(hh, s_ref[slot, hh], smax_ref[slot, hh], vt_ref[hh, j])

    def update_diagonal(j, slot, diag_block):
        live = (diag_block + 1) * BLOCK
        k_in_block = lax.broadcasted_iota(jnp.int32, (BLOCK, lanes), 0)
        q_in_block = lax.broadcasted_iota(jnp.int32, (BLOCK, lanes), 1)
        q_in_block = jnp.where(q_in_block >= tq, q_in_block - tq, q_in_block)
        for hh in range(heads):
            s = s_ref[slot, hh, 0:live, :]
            s_diag = jnp.where(k_in_block <= q_in_block, s[live - BLOCK:live], NEG_INF)
            if diag_block > 0:
                s = jnp.concatenate([s[0:live - BLOCK], s_diag], axis=0)
            else:
                s = s_diag
            accumulate(hh, s, jnp.max(s, axis=0, keepdims=True), vt_ref[hh, j, :, 0:live])

    m_ref[...] = jnp.full_like(m_ref, NEG_INF)
    acc_ref[...] = jnp.zeros_like(acc_ref)

    scores(j_first, 0)

    def body2(i, carry):
        j = j_first + 2 * i
        scores(j + 1, 1)
        update(j, 0)
        scores(j + 2, 0)
        update(j + 1, 1)
        return carry

    def body1(j, carry):
        scores(j + 1, 1)
        update(j, 0)
        return carry

    pairs = lax.shift_right_logical(n_full - j_first, 1)
    lax.fori_loop(0, pairs, body2, 0)
    lax.fori_loop(j_first + 2 * pairs, n_full, body1, 0)
    blocks_per_tile = tk // BLOCK
    diag_block = (q0 - n_full * tk) // BLOCK
    diag_slot = n_full - j_first - 2 * pairs
    for c in range(blocks_per_tile):
        pl.when(diag_block == c)(functools.partial(update_diagonal, n_full, diag_slot, c))

    for hh in range(heads):
        o = acc_ref[hh, 0:DIFF_VD] / acc_ref[hh, DIFF_VD:DIFF_VD + 1]
        dlt = o[:, 0:tq] - lam * o[:, tq:lanes]
        ms = jnp.mean(dlt * dlt, axis=0, keepdims=True)
        y = dlt * lax.rsqrt(ms + EPS) * sw_ref[...] * (1.0 - lami_ref[...])
        o_ref[:, hh * DIFF_VD:(hh + 1) * DIFF_VD] = y.T.astype(o_ref.dtype)


def _diff_attention(proj, lam_vecs, lam_init, subln_w, batch, p_rows):
    n = proj.shape[0]
    tq = BLOCK
    tk = ATTN_KEY_TILE
    heads = ATTN_HEADS_PER_STEP
    nq = p_rows // tq
    hw = 2 * DIFF_HD
    k_aug, v_t, first_tile = _kvprep(proj, batch, p_rows, tk)
    first_tile = first_tile[:, :nq, 0].reshape(batch * DIFF_HEADS * nq)
    n_tiles = k_aug.shape[1]
    groups = DIFF_HEADS // heads
    grid_spec = pltpu.PrefetchScalarGridSpec(
        num_scalar_prefetch=1,
        grid=(batch, groups, nq),
        in_specs=[
            pl.BlockSpec((tq, heads * hw), lambda b, g, i, first: (b * nq + i, COL_DQ // (heads * hw) + g)),
            pl.BlockSpec((heads, n_tiles, tk, 2 * hw), lambda b, g, i, first: (b * groups + g, 0, 0, 0),
                         pipeline_mode=pl.Buffered(1)),
            pl.BlockSpec((heads, n_tiles, VT_ROWS, tk), lambda b, g, i, first: (b * groups + g, 0, 0, 0),
                         pipeline_mode=pl.Buffered(1)),
            pl.BlockSpec((4, DIFF_HD), lambda b, g, i, first: (0, 0)),
            pl.BlockSpec((1, 1), lambda b, g, i, first: (0, 0)),
            pl.BlockSpec((DIFF_VD, 1), lambda b, g, i, first: (0, 0)),
        ],
        out_specs=pl.BlockSpec((tq, heads * DIFF_VD), lambda b, g, i, first: (b * nq + i, g)),
        scratch_shapes=[
            pltpu.VMEM((2, heads, tk, 2 * tq), F32),
            pltpu.VMEM((2, heads, 1, 2 * tq), F32),
            pltpu.VMEM((heads, 1, 2 * tq), F32),
            pltpu.VMEM((heads, VT_ROWS, 2 * tq), F32),
        ],
    )
    return pl.pallas_call(
        functools.partial(_attn_kernel, tq=tq, tk=tk, heads=heads),
        grid_spec=grid_spec,
        out_shape=jax.ShapeDtypeStruct((n, D_MODEL), BF16),
        compiler_params=_params("parallel", "parallel", "arbitrary"),
        name="diffattn",
    )(first_tile, proj, k_aug, v_t, lam_vecs, lam_init, subln_w)


def _merge_kernel(ret_ref, diff_ref, cb_ref, cc_ref, cx_ref, cch_ref, cxh_ref, g_ref, x_ref,
                  cw_ref, wb_ref, wo_ref, o_ref, *, tm, p_rows):
    start = pl.program_id(0) * tm
    rowi = lax.broadcasted_iota(jnp.int32, (tm, D_MODEL), 0)

    u = cc_ref[...].astype(F32) * cx_ref[...].astype(F32)
    uh = cch_ref[...].astype(F32) * cxh_ref[...].astype(F32)
    h1 = uh[BF16_SUBLANES - 1:BF16_SUBLANES, :]
    h2 = uh[BF16_SUBLANES - 2:BF16_SUBLANES - 1, :]
    u1 = jnp.where(rowi == 0, h1, pltpu.roll(u, 1, axis=0))
    u2 = jnp.where(rowi == 0, h2, jnp.where(rowi == 1, h1, pltpu.roll(u, 2, axis=0)))
    cw = cw_ref[...]
    conv = cb_ref[...].astype(F32) * (cw[0:1] * u2 + cw[1:2] * u1 + cw[2:3] * u)

    g = g_ref[...].astype(F32)
    g = 1.0 / (1.0 + jnp.exp(-g))
    merged = (g[:, 0:D_MODEL] * jnp.dot(ret_ref[...], wb_ref[0], preferred_element_type=F32)
              + g[:, D_MODEL:2 * D_MODEL] * jnp.dot(diff_ref[...], wb_ref[1], preferred_element_type=F32)
              + g[:, 2 * D_MODEL:] * jnp.dot(conv.astype(BF16), wb_ref[2], preferred_element_type=F32))
    y = x_ref[...] + jnp.dot(merged.astype(BF16), wo_ref[...], preferred_element_type=F32)
    r = rowi + (start - (start // p_rows) * p_rows)
    is_pad = (r < META_PAD) | ((r >= p_rows) & (r < p_rows + META_PAD))
    o_ref[...] = jnp.where(is_pad, 0.0, y)


def _merge(ret, diff, proj, x, conv_w, w_branch, w_out, p_rows):
    n, d = x.shape
    tm = _largest_divisor(n, (512, 256, 128))
    halo_blocks = tm // BF16_SUBLANES

    def col(c):
        return lambda i: (i, c // d)

    def halo(c):
        return lambda i: (jnp.maximum(i * halo_blocks - 1, 0), c // d)

    return pl.pallas_call(
        functools.partial(_merge_kernel, tm=tm, p_rows=p_rows),
        grid=(n // tm,),
        in_specs=[
            pl.BlockSpec((tm, d), lambda i: (i, 0)),
            pl.BlockSpec((tm, d), lambda i: (i, 0)),
            pl.BlockSpec((tm, d), col(COL_CB)),
            pl.BlockSpec((tm, d), col(COL_CC)),
            pl.BlockSpec((tm, d), col(COL_CX)),
            pl.BlockSpec((BF16_SUBLANES, d), halo(COL_CC)),
            pl.BlockSpec((BF16_SUBLANES, d), halo(COL_CX)),
            pl.BlockSpec((tm, N_BRANCH * d), lambda i: (i, COL_GATE // (N_BRANCH * d))),
            pl.BlockSpec((tm, d), lambda i: (i, 0)),
            pl.BlockSpec((CONV_K, d), lambda i: (0, 0)),
            pl.BlockSpec((N_BRANCH, d, d), lambda i: (0, 0, 0), pipeline_mode=pl.Buffered(1)),
            pl.BlockSpec((d, d), lambda i: (0, 0), pipeline_mode=pl.Buffered(1)),
        ],
        out_specs=pl.BlockSpec((tm, d), lambda i: (i, 0)),
        out_shape=jax.ShapeDtypeStruct((n, d), F32),
        compiler_params=_params("parallel"),
        name="merge",
    )(ret, diff, proj, proj, proj, proj, proj, proj, x, conv_w, w_branch, w_out)


def _mlp_kernel(x_ref, nw_ref, wu_ref, wd_ref, o_ref, *, ff_chunk):
    x = x_ref[...]
    xn = _rms(x, nw_ref[...]).astype(BF16)
    acc = x
    for c in range(wu_ref.shape[1] // ff_chunk):
        cols = slice(c * ff_chunk, (c + 1) * ff_chunk)
        u = jnp.maximum(jnp.dot(xn, wu_ref[:, cols], preferred_element_type=F32), 0.0)
        acc = acc + jnp.dot((u * u).astype(BF16), wd_ref[cols, :], preferred_element_type=F32)
    o_ref[...] = acc


def _mlp(x, norm_w, w_up, w_down):
    n, d = x.shape
    d_ff = w_up.shape[1]
    tm = _largest_divisor(n, (512, 256, 128))
    return pl.pallas_call(
        functools.partial(_mlp_kernel, ff_chunk=1024),
        grid=(n // tm,),
        in_specs=[
            pl.BlockSpec((tm, d), lambda i: (i, 0)),
            pl.BlockSpec((1, d), lambda i: (0, 0)),
            pl.BlockSpec((d, d_ff), lambda i: (0, 0)),
            pl.BlockSpec((d_ff, d), lambda i: (0, 0)),
        ],
        out_specs=pl.BlockSpec((tm, d), lambda i: (i, 0)),
        out_shape=jax.ShapeDtypeStruct((n, d), F32),
        compiler_params=_params("parallel"),
        name="mlp",
    )(x, norm_w, w_up, w_down)


def _final_norm_kernel(x_ref, w_ref, o_ref):
    o_ref[0] = _rms(x_ref[0], w_ref[...])


def _final_norm(h3, w, seq):
    batch, _, d = h3.shape
    return pl.pallas_call(
        _final_norm_kernel,
        grid=(batch, seq // BLOCK),
        in_specs=[
            pl.BlockSpec((1, BLOCK, d), lambda b, i: (b, i + 1, 0)),
            pl.BlockSpec((1, d), lambda b, i: (0, 0)),
        ],
        out_specs=pl.BlockSpec((1, BLOCK, d), lambda b, i: (b, i, 0)),
        out_shape=jax.ShapeDtypeStruct((batch, seq, d), h3.dtype),
        compiler_params=_params("parallel", "parallel"),
        name="final_norm",
    )(h3, w)


def kernel(x, meta_tokens, norm1_w, w_in, conv_w, diff_lambda, diff_subln_w, w_branch, w_out,
           norm2_w, w_up, w_down, final_norm_w):
    batch, seq, d = x.shape
    depth = w_in.shape[0]
    assert d == D_MODEL and w_in.shape[1:] == (D_MODEL, D_IN) and seq % BLOCK == 0
    p_rows = META_PAD + N_META + seq

    meta = jnp.broadcast_to(meta_tokens.astype(x.dtype)[None], (batch, N_META, d))
    h = jnp.concatenate([jnp.zeros((batch, META_PAD, d), x.dtype), meta, x], axis=1).reshape(batch * p_rows, d)

    layer_ids = jnp.arange(depth, dtype=F32)
    lam_inits = (0.8 - 0.6 * jnp.exp(-0.3 * layer_ids)).reshape(depth, 1, 1)
    w_in = w_in.at[:, :, COL_DQ:COL_DK].multiply(DIFF_HD ** -0.5 * LOG2E)
    layers = dict(
        norm1_w=norm1_w.reshape(depth, 1, d), w_in=w_in.astype(BF16), conv_w=conv_w,
        lam_vecs=diff_lambda.astype(F32), lam_init=lam_inits, subln_w=diff_subln_w.reshape(depth, DIFF_VD, 1),
        w_branch=w_branch.astype(BF16), w_out=w_out.astype(BF16), norm2_w=norm2_w.reshape(depth, 1, d),
        w_up=w_up.astype(BF16), w_down=w_down.astype(BF16))

    def layer(h, p):
        proj = _inproj(h, p["norm1_w"], p["w_in"])
        ret = _retention(proj, batch, p_rows)
        diff = _diff_attention(proj, p["lam_vecs"], p["lam_init"], p["subln_w"], batch, p_rows)
        h = _merge(ret, diff, proj, h, p["conv_w"], p["w_branch"], p["w_out"], p_rows)
        h = _mlp(h, p["norm2_w"], p["w_up"], p["w_down"])
        return h, None

    h, _ = lax.scan(layer, h, layers)
    return _final_norm(h.reshape(batch, p_rows, d), final_norm_w.reshape(1, d), seq)
```

```python
import functools
import math

import jax
import jax.numpy as jnp
from jax import lax
from jax.experimental import pallas as pl
from jax.experimental.pallas import tpu as pltpu

N_META = 16
BLOCK = 128
META_PAD = BLOCK - N_META
RET_HEADS = 4
RET_DK = 128
RET_DV = 256
DIFF_HEADS = 8
DIFF_HD = 64
DIFF_VD = 128
CONV_K = 3
N_BRANCH = 3
EPS = 1e-6
NEG_INF = -1e30

D_MODEL = RET_HEADS * RET_DV
COL_RQ, COL_RK, COL_RV, COL_RG = 0, 512, 1024, 2048
COL_DQ, COL_DK, COL_DV = 3072, 4096, 5120
COL_CB, COL_CC, COL_CX, COL_GATE = 6144, 7168, 8192, 9216
D_IN = COL_GATE + N_BRANCH * D_MODEL

V7X_VMEM_LIMIT_BYTES = 56 * 1024 * 1024
BF16_SUBLANES = 16
ATTN_KEY_TILE = 512
ATTN_HEADS_PER_STEP = 4

F32 = jnp.float32
BF16 = jnp.bfloat16


def _largest_divisor(n, candidates):
    for c in candidates:
        if n % c == 0:
            return c
    raise ValueError(f"no tile in {candidates} divides {n}")


def _params(*semantics):
    return pltpu.CompilerParams(dimension_semantics=semantics, vmem_limit_bytes=V7X_VMEM_LIMIT_BYTES)


def _rms(x, w):
    return x * lax.rsqrt(jnp.mean(x * x, axis=-1, keepdims=True) + EPS) * w


def _inproj_kernel(x_ref, nw_ref, w_ref, o_ref, xn_ref):
    @pl.when(pl.program_id(1) == 0)
    def _():
        xn_ref[...] = _rms(x_ref[...], nw_ref[...]).astype(BF16)

    o_ref[...] = jnp.dot(xn_ref[...], w_ref[...], preferred_element_type=F32).astype(o_ref.dtype)


def _inproj(h, norm_w, w_in):
    n, d = h.shape
    tm = _largest_divisor(n, (1024, 512, 256, 128))
    tn = 2048
    return pl.pallas_call(
        _inproj_kernel,
        grid=(n // tm, D_IN // tn),
        in_specs=[
            pl.BlockSpec((tm, d), lambda i, j: (i, 0)),
            pl.BlockSpec((1, d), lambda i, j: (0, 0)),
            pl.BlockSpec((d, tn), lambda i, j: (0, j)),
        ],
        out_specs=pl.BlockSpec((tm, tn), lambda i, j: (i, j)),
        out_shape=jax.ShapeDtypeStruct((n, D_IN), BF16),
        scratch_shapes=[pltpu.VMEM((tm, d), BF16)],
        compiler_params=_params("parallel", "arbitrary"),
        name="inproj",
    )(h, norm_w, w_in)


def _ret_log_gamma(head):
    return math.log1p(-(2.0 ** (-5.0 - head)))


def _retention_kernel(q_ref, k_ref, v_ref, g_ref, o_ref, state_ref, intra_ref, qd_ref, kd_ref, *, chunks):
    @pl.when(pl.program_id(1) == 0)
    def _():
        state_ref[...] = jnp.zeros_like(state_ref)
        scale = RET_DK ** -0.5
        i_sq = lax.broadcasted_iota(jnp.int32, (BLOCK, BLOCK), 0).astype(F32)
        j_sq = lax.broadcasted_iota(jnp.int32, (BLOCK, BLOCK), 1).astype(F32)
        i_dv = lax.broadcasted_iota(jnp.int32, (BLOCK, RET_DV), 0).astype(F32)
        dist = i_sq - j_sq
        for h in range(RET_HEADS):
            lg = _ret_log_gamma(h)
            intra_ref[h] = jnp.where(dist >= 0, jnp.exp(lg * jnp.maximum(dist, 0.0)), 0.0) * scale
            qd_ref[h] = jnp.exp(lg * (i_dv + 1.0))
            kd_ref[h] = jnp.exp(lg * (BLOCK - 1.0 - i_sq)) * scale

    for c in range(chunks):
        rows = pl.ds(c * BLOCK, BLOCK)
        for h in range(RET_HEADS):
            s_decay = math.exp(_ret_log_gamma(h) * BLOCK)
            q = q_ref[rows, h * RET_DK:(h + 1) * RET_DK]
            k = k_ref[rows, h * RET_DK:(h + 1) * RET_DK]
            v = v_ref[rows, h * RET_DV:(h + 1) * RET_DV]
            st = state_ref[h]
            scores = lax.dot_general(q, k, (((1,), (1,)), ((), ())), preferred_element_type=F32)
            scores = (scores * intra_ref[h]).astype(BF16)
            out = (jnp.dot(scores, v, preferred_element_type=F32)
                   + qd_ref[h] * jnp.dot(q, st.astype(BF16), preferred_element_type=F32))
            kdec = (k.astype(F32) * kd_ref[h]).astype(BF16)
            state_ref[h] = st * s_decay + lax.dot_general(
                kdec, v, (((0,), (0,)), ((), ())), preferred_element_type=F32)
            mu = jnp.mean(out, axis=-1, keepdims=True)
            cen = out - mu
            var = jnp.mean(cen * cen, axis=-1, keepdims=True)
            g = g_ref[rows, h * RET_DV:(h + 1) * RET_DV].astype(F32)
            silu = g / (1.0 + jnp.exp(-g))
            o_ref[rows, h * RET_DV:(h + 1) * RET_DV] = (cen * lax.rsqrt(var + EPS) * silu).astype(o_ref.dtype)


def _retention(proj, batch, p_rows):
    n = proj.shape[0]
    nc = p_rows // BLOCK
    chunks = _largest_divisor(nc, (5, 3, 1))
    steps = nc // chunks
    tr = chunks * BLOCK
    qk_w = RET_HEADS * RET_DK

    def rows(b, c):
        return b * steps + c

    return pl.pallas_call(
        functools.partial(_retention_kernel, chunks=chunks),
        grid=(batch, steps),
        in_specs=[
            pl.BlockSpec((tr, qk_w), lambda b, c: (rows(b, c), COL_RQ // qk_w)),
            pl.BlockSpec((tr, qk_w), lambda b, c: (rows(b, c), COL_RK // qk_w)),
            pl.BlockSpec((tr, D_MODEL), lambda b, c: (rows(b, c), COL_RV // D_MODEL)),
            pl.BlockSpec((tr, D_MODEL), lambda b, c: (rows(b, c), COL_RG // D_MODEL)),
        ],
        out_specs=pl.BlockSpec((tr, D_MODEL), lambda b, c: (rows(b, c), 0)),
        out_shape=jax.ShapeDtypeStruct((n, D_MODEL), BF16),
        scratch_shapes=[
            pltpu.VMEM((RET_HEADS, RET_DK, RET_DV), F32),
            pltpu.VMEM((RET_HEADS, BLOCK, BLOCK), F32),
            pltpu.VMEM((RET_HEADS, BLOCK, RET_DV), F32),
            pltpu.VMEM((RET_HEADS, BLOCK, BLOCK), F32),
        ],
        compiler_params=_params("parallel", "arbitrary"),
        name="retention",
    )(proj, proj, proj, proj)


FEAT_BLK, FEAT_ROW, FEAT_MASK, FEAT_Q0, FEAT_QROW = 0, 3, 6, 7, 10
FEAT_SPLIT = 3
LOG2E = math.log2(math.e)
VT_ROWS = DIFF_VD + 16
SKIP_LOG2_MARGIN = 160.0
SKIP_NORM_SLACK = 1.01


def _max_sq_norm(x, map_selector):
    sq = x.astype(F32)
    sq = (sq * sq).astype(BF16)
    per_map = jnp.dot(sq, map_selector, preferred_element_type=F32)
    return jnp.max(jnp.max(per_map, axis=0, keepdims=True), axis=-1, keepdims=True)


def _first_live_tile(q2, k2, head, *, tk, n_tiles):
    blocks_per_tile = tk // BLOCK
    qi = lax.broadcasted_iota(jnp.int32, (BLOCK, BLOCK), 0)
    tj = lax.broadcasted_iota(jnp.int32, (BLOCK, BLOCK), 1)
    diag_tile = lax.shift_right_logical(qi, blocks_per_tile.bit_length() - 1)
    slope2 = jnp.exp2(-(jnp.zeros((BLOCK, BLOCK), F32) + head + 1.0)) * LOG2E
    kd2 = jnp.max(jnp.where(tj == diag_tile, k2, 0.0), axis=-1, keepdims=True)
    dist_min = (qi * BLOCK - tj * tk - (tk - 1)).astype(F32)
    bound = SKIP_NORM_SLACK * jnp.sqrt(q2) * (jnp.sqrt(k2) + jnp.sqrt(kd2)) + SKIP_LOG2_MARGIN
    dead = jnp.where((slope2 * dist_min > bound) & (tj < diag_tile), 1.0, 0.0)
    run = jnp.ones((BLOCK, 1), F32)
    count = jnp.zeros((BLOCK, 1), F32)
    for j in range(n_tiles):
        run = run * dead[:, j:j + 1]
        count = count + run
    return count


def _kvprep_kernel(q_ref, k_ref, v_ref, k_out_ref, vt_ref, js_ref, *, tk, n_real, n_tiles):
    blocks_per_tile = tk // BLOCK
    hw = 2 * DIFF_HD
    lane = lax.broadcasted_iota(jnp.int32, (BLOCK, hw), 1)
    row = lax.broadcasted_iota(jnp.int32, (BLOCK, hw), 0)
    map_selector = jnp.where(((lane == 0) & (row < DIFF_HD)) | ((lane == 1) & (row >= DIFF_HD)), 1.0, 0.0)
    map_selector = map_selector.astype(BF16)
    ones_row = jnp.where(lax.broadcasted_iota(jnp.int32, (VT_ROWS - DIFF_VD, BLOCK), 0) == 0, 1.0, 0.0)
    q2 = jnp.zeros((BLOCK, BLOCK), F32)
    k2 = jnp.zeros((BLOCK, BLOCK), F32)
    for t in range(n_tiles):
        for bi in range(blocks_per_tile):
            blk = t * blocks_per_tile + bi
            rows = pl.ds(bi * BLOCK, BLOCK)
            cols = pl.ds(bi * BLOCK, BLOCK)
            vt_ref[0, t, DIFF_VD:, cols] = ones_row.astype(BF16)
            if blk < n_real:
                src = slice(blk * BLOCK, (blk + 1) * BLOCK)
                kb = k_ref[src, :]
                k_out_ref[0, t, rows, :] = kb
                vt_ref[0, t, 0:DIFF_VD, cols] = v_ref[src, :].astype(F32).T.astype(BF16)
                q2 = jnp.where(row == blk, _max_sq_norm(q_ref[src, :], map_selector), q2)
                k2 = jnp.where(lane == t, jnp.maximum(k2, _max_sq_norm(kb, map_selector)), k2)
            else:
                k_out_ref[0, t, rows, :] = jnp.zeros((BLOCK, hw), BF16)
                vt_ref[0, t, 0:DIFF_VD, cols] = jnp.zeros((DIFF_VD, BLOCK), BF16)
    first = _first_live_tile(q2, k2, pl.program_id(1).astype(F32), tk=tk, n_tiles=n_tiles)
    js_ref[0] = jnp.broadcast_to(first, (BLOCK, BLOCK)).astype(jnp.int32)


def _kvprep(proj, batch, p_rows, tk):
    n_real = p_rows // BLOCK
    n_tiles = -(-p_rows // tk)
    hw = 2 * DIFF_HD
    blocks_per_tile = tk // BLOCK
    assert hw == BLOCK and n_real <= BLOCK and blocks_per_tile & (blocks_per_tile - 1) == 0
    return pl.pallas_call(
        functools.partial(_kvprep_kernel, tk=tk, n_real=n_real, n_tiles=n_tiles),
        grid=(batch, DIFF_HEADS),
        in_specs=[
            pl.BlockSpec((p_rows, hw), lambda b, h: (b, COL_DQ // hw + h)),
            pl.BlockSpec((p_rows, hw), lambda b, h: (b, COL_DK // hw + h)),
            pl.BlockSpec((p_rows, DIFF_VD), lambda b, h: (b, COL_DV // DIFF_VD + h)),
        ],
        out_specs=[
            pl.BlockSpec((1, n_tiles, tk, hw), lambda b, h: (b * DIFF_HEADS + h, 0, 0, 0)),
            pl.BlockSpec((1, n_tiles, VT_ROWS, tk), lambda b, h: (b * DIFF_HEADS + h, 0, 0, 0)),
            pl.BlockSpec((1, BLOCK, BLOCK), lambda b, h: (b * DIFF_HEADS + h, 0, 0)),
        ],
        out_shape=[
            jax.ShapeDtypeStruct((batch * DIFF_HEADS, n_tiles, tk, hw), BF16),
            jax.ShapeDtypeStruct((batch * DIFF_HEADS, n_tiles, VT_ROWS, tk), BF16),
            jax.ShapeDtypeStruct((batch * DIFF_HEADS, BLOCK, BLOCK), jnp.int32),
        ],
        compiler_params=_params("parallel", "parallel"),
        name="kvprep",
    )(proj, proj, proj)


def _attn_kernel(first_ref, q_ref, k_ref, vt_ref, feat_ref, lamv_ref, lami_ref, sw_ref, o_ref,
                 s_ref, smax_ref, m_ref, acc_ref, *, tq, tk, heads):
    hp = pl.program_id(1)
    qi = pl.program_id(2)
    q0 = qi * tq
    lanes = 2 * tq
    n_full = q0 // tk
    hw = 2 * DIFF_HD
    head0 = pl.program_id(0) * DIFF_HEADS + hp * heads
    j_first = n_full
    for hh in range(heads):
        j_first = jnp.minimum(j_first, first_ref[(head0 + hh) * pl.num_programs(2) + qi])

    lane_h = lax.broadcasted_iota(jnp.int32, (tq, hw), 1)
    row_q = lax.broadcasted_iota(jnp.int32, (tq, hw), 0).astype(F32)
    lv = lamv_ref[...]
    lam = (jnp.exp(jnp.sum(lv[0:1] * lv[1:2], axis=-1, keepdims=True))
           - jnp.exp(jnp.sum(lv[2:3] * lv[3:4], axis=-1, keepdims=True)) + lami_ref[...])

    def split_into(feat, first_lane, x):
        for piece in range(FEAT_SPLIT):
            head_part = x.astype(BF16).astype(F32)
            feat = jnp.where(lane_h == first_lane + piece, head_part, feat)
            x = x - head_part
        return feat

    def qcat_of(hh):
        head = (hp * heads + hh).astype(F32)
        slope2 = jnp.exp2(-(jnp.zeros((tq, hw), F32) + head + 1.0)) * LOG2E
        feat = jnp.where(lane_h == FEAT_MASK, NEG_INF, 0.0)
        feat = split_into(feat, FEAT_BLK, slope2 * BLOCK)
        feat = split_into(feat, FEAT_ROW, slope2)
        feat = split_into(feat, FEAT_Q0, -slope2 * q0.astype(F32))
        feat = split_into(feat, FEAT_QROW, -slope2 * row_q)
        q = q_ref[:, hh * hw:(hh + 1) * hw].astype(F32)
        q_m0 = jnp.where(lane_h < DIFF_HD, q, 0.0)
        q_m1 = jnp.where(lane_h >= DIFF_HD, q, 0.0)
        top = jnp.concatenate([q_m0.T, q_m1.T], axis=1)
        bottom = jnp.concatenate([feat.T, feat.T], axis=1)
        return jnp.concatenate([top, bottom], axis=0).astype(BF16)

    qcats = [qcat_of(hh) for hh in range(heads)]

    def scores(j, slot):
        for hh in range(heads):
            k_aug = jnp.concatenate([k_ref[hh, j], feat_ref[j]], axis=1)
            s = jnp.dot(k_aug, qcats[hh], preferred_element_type=F32)
            s_ref[slot, hh] = s
            smax_ref[slot, hh] = jnp.max(s, axis=0, keepdims=True)

    def accumulate(hh, s, s_max, vt):
        m_old = m_ref[hh]
        m_new = jnp.maximum(m_old, s_max)
        alpha = jnp.exp2(m_old - m_new)
        p = jnp.exp2(s - m_new).astype(BF16)
        m_ref[hh] = m_new
        pv = jnp.dot(vt, p, preferred_element_type=F32)
        acc_ref[hh] = acc_ref[hh] * alpha + pv

    def update(j, slot):
        for hh in range(heads):
            accumulate(hh, s_ref[slot, hh], smax_ref[slot, hh], vt_ref[hh, j])

    def update_diagonal(j, slot, diag_block):
        live = (diag_block + 1) * BLOCK
        k_in_block = lax.broadcasted_iota(jnp.int32, (BLOCK, lanes), 0)
        q_in_block = lax.broadcasted_iota(jnp.int32, (BLOCK, lanes), 1)
        q_in_block = jnp.where(q_in_block >= tq, q_in_block - tq, q_in_block)
        for hh in range(heads):
            s = s_ref[slot, hh, 0:live, :]
            s_diag = jnp.where(k_in_block <= q_in_block, s[live - BLOCK:live], NEG_INF)
            if diag_block > 0:
                s = jnp.concatenate([s[0:live - BLOCK], s_diag], axis=0)
            else:
                s = s_diag
            accumulate(hh, s, jnp.max(s, axis=0, keepdims=True), vt_ref[hh, j, :, 0:live])
            o = acc_ref[hh, 0:DIFF_VD] / acc_ref[hh, DIFF_VD:DIFF_VD + 1]
            dlt = o[:, 0:tq] - lam * o[:, tq:lanes]
            ms = jnp.mean(dlt * dlt, axis=0, keepdims=True)
            y = dlt * lax.rsqrt(ms + EPS) * sw_ref[...] * (1.0 - lami_ref[...])
            o_ref[:, hh * DIFF_VD:(hh + 1) * DIFF_VD] = y.T.astype(o_ref.dtype)

    m_ref[...] = jnp.full_like(m_ref, NEG_INF)
    acc_ref[...] = jnp.zeros_like(acc_ref)

    scores(j_first, 0)

    def tiles(j, count):
        for t in range(count):
            scores(j + t + 1, (t + 1) % 2)
            update(j + t, t % 2)

    def body4(i, carry):
        tiles(j_first + 4 * i, 4)
        return carry

    def body2(j, carry):
        tiles(j, 2)
        return carry

    def body1(j, carry):
        tiles(j, 1)
        return carry

    n_loop = n_full - j_first
    quads = lax.shift_right_logical(n_loop, 2)
    j_pair = j_first + 4 * quads
    j_single = j_pair + (n_loop & 2)
    lax.fori_loop(0, quads, body4, 0)
    lax.fori_loop(j_pair, j_pair + lax.shift_right_logical(n_loop & 2, 1), body2, 0)
    lax.fori_loop(j_single, n_full, body1, 0)
    blocks_per_tile = tk // BLOCK
    diag_block = (q0 - n_full * tk) // BLOCK
    diag_slot = n_loop & 1
    for c in range(blocks_per_tile):
        pl.when(diag_block == c)(functools.partial(update_diagonal, n_full, diag_slot, c))


def _position_features(n_tiles, tk, p_rows):
    kpos = jnp.arange(n_tiles * tk, dtype=jnp.int32)[:, None]
    lane = jnp.arange(2 * DIFF_HD, dtype=jnp.int32)[None, :]
    masked = (kpos < META_PAD) | (kpos >= p_rows)
    feat = jnp.where(lane < FEAT_ROW, kpos // BLOCK,
           jnp.where(lane < FEAT_MASK, kpos % BLOCK,
           jnp.where(lane == FEAT_MASK, masked.astype(jnp.int32),
           jnp.where(lane < FEAT_QROW + FEAT_SPLIT, 1, 0))))
    return feat.astype(BF16).reshape(n_tiles, tk, 2 * DIFF_HD)


def _diff_attention(proj, lam_vecs, lam_init, subln_w, batch, p_rows):
    n = proj.shape[0]
    tq = BLOCK
    tk = ATTN_KEY_TILE
    heads = ATTN_HEADS_PER_STEP
    nq = p_rows // tq
    hw = 2 * DIFF_HD
    k_tiles, v_t, first_tile = _kvprep(proj, batch, p_rows, tk)
    first_tile = first_tile[:, :nq, 0].reshape(batch * DIFF_HEADS * nq)
    n_tiles = k_tiles.shape[1]
    feat = _position_features(n_tiles, tk, p_rows)
    groups = DIFF_HEADS // heads
    grid_spec = pltpu.PrefetchScalarGridSpec(
        num_scalar_prefetch=1,
        grid=(batch, groups, nq),
        in_specs=[
            pl.BlockSpec((tq, heads * hw), lambda b, g, i, first: (b * nq + i, COL_DQ // (heads * hw) + g)),
            pl.BlockSpec((heads, n_tiles, tk, hw), lambda b, g, i, first: (b * groups + g, 0, 0, 0)),
            pl.BlockSpec((heads, n_tiles, VT_ROWS, tk), lambda b, g, i, first: (b * groups + g, 0, 0, 0)),
            pl.BlockSpec((n_tiles, tk, hw), lambda b, g, i, first: (0, 0, 0), pipeline_mode=pl.Buffered(1)),
            pl.BlockSpec((4, DIFF_HD), lambda b, g, i, first: (0, 0)),
            pl.BlockSpec((1, 1), lambda b, g, i, first: (0, 0)),
            pl.BlockSpec((DIFF_VD, 1), lambda b, g, i, first: (0, 0)),
        ],
        out_specs=pl.BlockSpec((tq, heads * DIFF_VD), lambda b, g, i, first: (b * nq + i, g)),
        scratch_shapes=[
            pltpu.VMEM((2, heads, tk, 2 * tq), F32),
            pltpu.VMEM((2, heads, 1, 2 * tq), F32),
            pltpu.VMEM((heads, 1, 2 * tq), F32),
            pltpu.VMEM((heads, VT_ROWS, 2 * tq), F32),
        ],
    )
    return pl.pallas_call(
        functools.partial(_attn_kernel, tq=tq, tk=tk, heads=heads),
        grid_spec=grid_spec,
        out_shape=jax.ShapeDtypeStruct((n, D_MODEL), BF16),
        compiler_params=_params("parallel", "parallel", "arbitrary"),
        name="diffattn",
    )(first_tile, proj, k_tiles, v_t, feat, lam_vecs, lam_init, subln_w)


def _merge_kernel(ret_ref, diff_ref, cb_ref, cc_ref, cx_ref, cch_ref, cxh_ref, g_ref, x_ref,
                  cw_ref, wb_ref, wo_ref, o_ref, *, tm, p_rows):
    start = pl.program_id(0) * tm
    rowi = lax.broadcasted_iota(jnp.int32, (tm, D_MODEL), 0)

    u = cc_ref[...].astype(F32) * cx_ref[...].astype(F32)
    uh = cch_ref[...].astype(F32) * cxh_ref[...].astype(F32)
    h1 = uh[BF16_SUBLANES - 1:BF16_SUBLANES, :]
    h2 = uh[BF16_SUBLANES - 2:BF16_SUBLANES - 1, :]
    u1 = jnp.where(rowi == 0, h1, pltpu.roll(u, 1, axis=0))
    u2 = jnp.where(rowi == 0, h2, jnp.where(rowi == 1, h1, pltpu.roll(u, 2, axis=0)))
    cw = cw_ref[...]
    conv = cb_ref[...].astype(F32) * (cw[0:1] * u2 + cw[1:2] * u1 + cw[2:3] * u)

    g = g_ref[...].astype(F32)
    g = 1.0 / (1.0 + jnp.exp(-g))
    merged = (g[:, 0:D_MODEL] * jnp.dot(ret_ref[...], wb_ref[0], preferred_element_type=F32)
              + g[:, D_MODEL:2 * D_MODEL] * jnp.dot(diff_ref[...], wb_ref[1], preferred_element_type=F32)
              + g[:, 2 * D_MODEL:] * jnp.dot(conv.astype(BF16), wb_ref[2], preferred_element_type=F32))
    y = x_ref[...] + jnp.dot(merged.astype(BF16), wo_ref[...], preferred_element_type=F32)
    r = rowi + (start - (start // p_rows) * p_rows)
    is_pad = (r < META_PAD) | ((r >= p_rows) & (r < p_rows + META_PAD))
    o_ref[...] = jnp.where(is_pad, 0.0, y)


def _merge(ret, diff, proj, x, conv_w, w_branch, w_out, p_rows):
    n, d = x.shape
    tm = _largest_divisor(n, (512, 256, 128))
    halo_blocks = tm // BF16_SUBLANES

    def col(c):
        return lambda i: (i, c // d)

    def halo(c):
        return lambda i: (jnp.maximum(i * halo_blocks - 1, 0), c // d)

    return pl.pallas_call(
        functools.partial(_merge_kernel, tm=tm, p_rows=p_rows),
        grid=(n // tm,),
        in_specs=[
            pl.BlockSpec((tm, d), lambda i: (i, 0)),
            pl.BlockSpec((tm, d), lambda i: (i, 0)),
            pl.BlockSpec((tm, d), col(COL_CB)),
            pl.BlockSpec((tm, d), col(COL_CC)),
            pl.BlockSpec((tm, d), col(COL_CX)),
            pl.BlockSpec((BF16_SUBLANES, d), halo(COL_CC)),
            pl.BlockSpec((BF16_SUBLANES, d), halo(COL_CX)),
            pl.BlockSpec((tm, N_BRANCH * d), lambda i: (i, COL_GATE // (N_BRANCH * d))),
            pl.BlockSpec((tm, d), lambda i: (i, 0)),
            pl.BlockSpec((CONV_K, d), lambda i: (0, 0)),
            pl.BlockSpec((N_BRANCH, d, d), lambda i: (0, 0, 0), pipeline_mode=pl.Buffered(1)),
            pl.BlockSpec((d, d), lambda i: (0, 0), pipeline_mode=pl.Buffered(1)),
        ],
        out_specs=pl.BlockSpec((tm, d), lambda i: (i, 0)),
        out_shape=jax.ShapeDtypeStruct((n, d), F32),
        compiler_params=_params("parallel"),
        name="merge",
    )(ret, diff, proj, proj, proj, proj, proj, proj, x, conv_w, w_branch, w_out)


def _mlp_kernel(x_ref, nw_ref, wu_ref, wd_ref, o_ref, *, ff_chunk):
    x = x_ref[...]
    xn = _rms(x, nw_ref[...]).astype(BF16)
    acc = x
    for c in range(wu_ref.shape[1] // ff_chunk):
        cols = slice(c * ff_chunk, (c + 1) * ff_chunk)
        u = jnp.maximum(jnp.dot(xn, wu_ref[:, cols], preferred_element_type=F32), 0.0)
        acc = acc + jnp.dot((u * u).astype(BF16), wd_ref[cols, :], preferred_element_type=F32)
    o_ref[...] = acc


def _mlp(x, norm_w, w_up, w_down):
    n, d = x.shape
    d_ff = w_up.shape[1]
    tm = _largest_divisor(n, (512, 256, 128))
    return pl.pallas_call(
        functools.partial(_mlp_kernel, ff_chunk=1024),
        grid=(n // tm,),
        in_specs=[
            pl.BlockSpec((tm, d), lambda i: (i, 0)),
            pl.BlockSpec((1, d), lambda i: (0, 0)),
            pl.BlockSpec((d, d_ff), lambda i: (0, 0)),
            pl.BlockSpec((d_ff, d), lambda i: (0, 0)),
        ],
        out_specs=pl.BlockSpec((tm, d), lambda i: (i, 0)),
        out_shape=jax.ShapeDtypeStruct((n, d), F32),
        compiler_params=_params("parallel"),
        name="mlp",
    )(x, norm_w, w_up, w_down)


def _final_norm_kernel(x_ref, w_ref, o_ref):
    o_ref[0] = _rms(x_ref[0], w_ref[...])


def _final_norm(h3, w, seq):
    batch, _, d = h3.shape
    return pl.pallas_call(
        _final_norm_kernel,
        grid=(batch, seq // BLOCK),
        in_specs=[
            pl.BlockSpec((1, BLOCK, d), lambda b, i: (b, i + 1, 0)),
            pl.BlockSpec((1, d), lambda b, i: (0, 0)),
        ],
        out_specs=pl.BlockSpec((1, BLOCK, d), lambda b, i: (b, i, 0)),
        out_shape=jax.ShapeDtypeStruct((batch, seq, d), h3.dtype),
        compiler_params=_params("parallel", "parallel"),
        name="final_norm",
    )(h3, w)


def kernel(x, meta_tokens, norm1_w, w_in, conv_w, diff_lambda, diff_subln_w, w_branch, w_out,
           norm2_w, w_up, w_down, final_norm_w):
    batch, seq, d = x.shape
    depth = w_in.shape[0]
    assert d == D_MODEL and w_in.shape[1:] == (D_MODEL, D_IN) and seq % BLOCK == 0
    p_rows = META_PAD + N_META + seq

    meta = jnp.broadcast_to(meta_tokens.astype(x.dtype)[None], (batch, N_META, d))
    h = jnp.concatenate([jnp.zeros((batch, META_PAD, d), x.dtype), meta, x], axis=1).reshape(batch * p_rows, d)

    layer_ids = jnp.arange(depth, dtype=F32)
    lam_inits = (0.8 - 0.6 * jnp.exp(-0.3 * layer_ids)).reshape(depth, 1, 1)
    w_in = w_in.at[:, :, COL_DQ:COL_DK].multiply(DIFF_HD ** -0.5 * LOG2E)
    layers = dict(
        norm1_w=norm1_w.reshape(depth, 1, d), w_in=w_in.astype(BF16), conv_w=conv_w,
        lam_vecs=diff_lambda.astype(F32), lam_init=lam_inits, subln_w=diff_subln_w.reshape(depth, DIFF_VD, 1),
        w_branch=w_branch.astype(BF16), w_out=w_out.astype(BF16), norm2_w=norm2_w.reshape(depth, 1, d),
        w_up=w_up.astype(BF16), w_down=w_down.astype(BF16))

    def layer(h, p):
        proj = _inproj(h, p["norm1_w"], p["w_in"])
        ret = _retention(proj, batch, p_rows)
        diff = _diff_attention(proj, p["lam_vecs"], p["lam_init"], p["subln_w"], batch, p_rows)
        h = _merge(ret, diff, proj, h, p["conv_w"], p["w_branch"], p["w_out"], p_rows)
        h = _mlp(h, p["norm2_w"], p["w_up"], p["w_down"])
        return h, None

    h, _ = lax.scan(layer, h, layers)
    return _final_norm(h.reshape(batch, p_rows, d), final_norm_w.reshape(1, d), seq)
```

```python
import functools
import math

import jax
import jax.numpy as jnp
from jax import lax
from jax.experimental import pallas as pl
from jax.experimental.pallas import tpu as pltpu

N_META = 16
BLOCK = 128
META_PAD = BLOCK - N_META
RET_HEADS = 4
RET_DK = 128
RET_DV = 256
DIFF_HEADS = 8
DIFF_HD = 64
DIFF_VD = 128
CONV_K = 3
N_BRANCH = 3
EPS = 1e-6
NEG_INF = -1e30

D_MODEL = RET_HEADS * RET_DV
COL_RQ, COL_RK, COL_RV, COL_RG = 0, 512, 1024, 2048
COL_DQ, COL_DK, COL_DV = 3072, 4096, 5120
COL_CB, COL_CC, COL_CX, COL_GATE = 6144, 7168, 8192, 9216
D_IN = COL_GATE + N_BRANCH * D_MODEL

V7X_VMEM_LIMIT_BYTES = 56 * 1024 * 1024
BF16_SUBLANES = 16
ATTN_KEY_TILE = 512
ATTN_HEADS_PER_STEP = 4

F32 = jnp.float32
BF16 = jnp.bfloat16


def _largest_divisor(n, candidates):
    for c in candidates:
        if n % c == 0:
            return c
    raise ValueError(f"no tile in {candidates} divides {n}")


def _params(*semantics):
    return pltpu.CompilerParams(dimension_semantics=semantics, vmem_limit_bytes=V7X_VMEM_LIMIT_BYTES)


def _rms(x, w):
    return x * lax.rsqrt(jnp.mean(x * x, axis=-1, keepdims=True) + EPS) * w


def _inproj_kernel(x_ref, nw_ref, w_ref, o_ref, xn_ref):
    @pl.when(pl.program_id(1) == 0)
    def _():
        xn_ref[...] = _rms(x_ref[...], nw_ref[...]).astype(BF16)

    o_ref[...] = jnp.dot(xn_ref[...], w_ref[...], preferred_element_type=F32).astype(o_ref.dtype)


def _inproj(h, norm_w, w_in):
    n, d = h.shape
    tm = _largest_divisor(n, (1024, 512, 256, 128))
    tn = 2048
    return pl.pallas_call(
        _inproj_kernel,
        grid=(n // tm, D_IN // tn),
        in_specs=[
            pl.BlockSpec((tm, d), lambda i, j: (i, 0)),
            pl.BlockSpec((1, d), lambda i, j: (0, 0)),
            pl.BlockSpec((d, tn), lambda i, j: (0, j)),
        ],
        out_specs=pl.BlockSpec((tm, tn), lambda i, j: (i, j)),
        out_shape=jax.ShapeDtypeStruct((n, D_IN), BF16),
        scratch_shapes=[pltpu.VMEM((tm, d), BF16)],
        compiler_params=_params("parallel", "arbitrary"),
        name="inproj",
    )(h, norm_w, w_in)


def _ret_log_gamma(head):
    return math.log1p(-(2.0 ** (-5.0 - head)))


def _retention_kernel(q_ref, k_ref, v_ref, g_ref, o_ref, state_ref, intra_ref, qd_ref, kd_ref, *, chunks):
    @pl.when(pl.program_id(1) == 0)
    def _():
        state_ref[...] = jnp.zeros_like(state_ref)
        scale = RET_DK ** -0.5
        i_sq = lax.broadcasted_iota(jnp.int32, (BLOCK, BLOCK), 0).astype(F32)
        j_sq = lax.broadcasted_iota(jnp.int32, (BLOCK, BLOCK), 1).astype(F32)
        i_dv = lax.broadcasted_iota(jnp.int32, (BLOCK, RET_DV), 0).astype(F32)
        dist = i_sq - j_sq
        for h in range(RET_HEADS):
            lg = _ret_log_gamma(h)
            intra_ref[h] = jnp.where(dist >= 0, jnp.exp(lg * jnp.maximum(dist, 0.0)), 0.0) * scale
            qd_ref[h] = jnp.exp(lg * (i_dv + 1.0))
            kd_ref[h] = jnp.exp(lg * (BLOCK - 1.0 - i_sq)) * scale

    for c in range(chunks):
        rows = pl.ds(c * BLOCK, BLOCK)
        for h in range(RET_HEADS):
            s_decay = math.exp(_ret_log_gamma(h) * BLOCK)
            q = q_ref[rows, h * RET_DK:(h + 1) * RET_DK]
            k = k_ref[rows, h * RET_DK:(h + 1) * RET_DK]
            v = v_ref[rows, h * RET_DV:(h + 1) * RET_DV]
            st = state_ref[h]
            scores = lax.dot_general(q, k, (((1,), (1,)), ((), ())), preferred_element_type=F32)
            scores = (scores * intra_ref[h]).astype(BF16)
            out = (jnp.dot(scores, v, preferred_element_type=F32)
                   + qd_ref[h] * jnp.dot(q, st.astype(BF16), preferred_element_type=F32))
            kdec = (k.astype(F32) * kd_ref[h]).astype(BF16)
            state_ref[h] = st * s_decay + lax.dot_general(
                kdec, v, (((0,), (0,)), ((), ())), preferred_element_type=F32)
            mu = jnp.mean(out, axis=-1, keepdims=True)
            cen = out - mu
            var = jnp.mean(cen * cen, axis=-1, keepdims=True)
            g = g_ref[rows, h * RET_DV:(h + 1) * RET_DV].astype(F32)
            silu = g / (1.0 + jnp.exp(-g))
            o_ref[rows, h * RET_DV:(h + 1) * RET_DV] = (cen * lax.rsqrt(var + EPS) * silu).astype(o_ref.dtype)


def _retention(proj, batch, p_rows):
    n = proj.shape[0]
    nc = p_rows // BLOCK
    chunks = _largest_divisor(nc, (5, 3, 1))
    steps = nc // chunks
    tr = chunks * BLOCK
    qk_w = RET_HEADS * RET_DK

    def rows(b, c):
        return b * steps + c

    return pl.pallas_call(
        functools.partial(_retention_kernel, chunks=chunks),
        grid=(batch, steps),
        in_specs=[
            pl.BlockSpec((tr, qk_w), lambda b, c: (rows(b, c), COL_RQ // qk_w)),
            pl.BlockSpec((tr, qk_w), lambda b, c: (rows(b, c), COL_RK // qk_w)),
            pl.BlockSpec((tr, D_MODEL), lambda b, c: (rows(b, c), COL_RV // D_MODEL)),
            pl.BlockSpec((tr, D_MODEL), lambda b, c: (rows(b, c), COL_RG // D_MODEL)),
        ],
        out_specs=pl.BlockSpec((tr, D_MODEL), lambda b, c: (rows(b, c), 0)),
        out_shape=jax.ShapeDtypeStruct((n, D_MODEL), BF16),
        scratch_shapes=[
            pltpu.VMEM((RET_HEADS, RET_DK, RET_DV), F32),
            pltpu.VMEM((RET_HEADS, BLOCK, BLOCK), F32),
            pltpu.VMEM((RET_HEADS, BLOCK, RET_DV), F32),
            pltpu.VMEM((RET_HEADS, BLOCK, BLOCK), F32),
        ],
        compiler_params=_params("parallel", "arbitrary"),
        name="retention",
    )(proj, proj, proj, proj)


FEAT_BLK, FEAT_ROW, FEAT_MASK, FEAT_Q0, FEAT_QROW = 0, 3, 6, 7, 10
FEAT_SPLIT = 3
LOG2E = math.log2(math.e)
VT_ROWS = DIFF_VD + BF16_SUBLANES
SKIP_LOG2_MARGIN = 160.0
SKIP_NORM_SLACK = 1.01


def _max_sq_norm(x, map_selector):
    sq = x.astype(F32)
    sq = (sq * sq).astype(BF16)
    per_map = jnp.dot(sq, map_selector, preferred_element_type=F32)
    return jnp.max(jnp.max(per_map, axis=0, keepdims=True), axis=-1, keepdims=True)


def _first_live_tile(q2, k2, head, *, tk, n_tiles):
    blocks_per_tile = tk // BLOCK
    qi = lax.broadcasted_iota(jnp.int32, (BLOCK, BLOCK), 0)
    tj = lax.broadcasted_iota(jnp.int32, (BLOCK, BLOCK), 1)
    diag_tile = lax.shift_right_logical(qi, blocks_per_tile.bit_length() - 1)
    slope2 = jnp.exp2(-(jnp.zeros((BLOCK, BLOCK), F32) + head + 1.0)) * LOG2E
    kd2 = jnp.max(jnp.where(tj == diag_tile, k2, 0.0), axis=-1, keepdims=True)
    dist_min = (qi * BLOCK - tj * tk - (tk - 1)).astype(F32)
    bound = SKIP_NORM_SLACK * jnp.sqrt(q2) * (jnp.sqrt(k2) + jnp.sqrt(kd2)) + SKIP_LOG2_MARGIN
    dead = jnp.where((slope2 * dist_min > bound) & (tj < diag_tile), 1.0, 0.0)
    run = jnp.ones((BLOCK, 1), F32)
    count = jnp.zeros((BLOCK, 1), F32)
    for j in range(n_tiles):
        run = run * dead[:, j:j + 1]
        count = count + run
    return count


def _kvprep_kernel(q_ref, k_ref, v_ref, k_out_ref, vt_ref, js_ref, *, tk, n_real, n_tiles):
    blocks_per_tile = tk // BLOCK
    hw = 2 * DIFF_HD
    lane = lax.broadcasted_iota(jnp.int32, (BLOCK, hw), 1)
    row = lax.broadcasted_iota(jnp.int32, (BLOCK, hw), 0)
    map_selector = jnp.where(((lane == 0) & (row < DIFF_HD)) | ((lane == 1) & (row >= DIFF_HD)), 1.0, 0.0)
    map_selector = map_selector.astype(BF16)
    ones_row = jnp.where(lax.broadcasted_iota(jnp.int32, (VT_ROWS - DIFF_VD, BLOCK), 0) == 0, 1.0, 0.0)
    q2 = jnp.zeros((BLOCK, BLOCK), F32)
    k2 = jnp.zeros((BLOCK, BLOCK), F32)
    for t in range(n_tiles):
        for bi in range(blocks_per_tile):
            blk = t * blocks_per_tile + bi
            rows = pl.ds(bi * BLOCK, BLOCK)
            cols = pl.ds(bi * BLOCK, BLOCK)
            vt_ref[0, t, DIFF_VD:, cols] = ones_row.astype(BF16)
            if blk < n_real:
                src = slice(blk * BLOCK, (blk + 1) * BLOCK)
                kb = k_ref[src, :]
                k_out_ref[0, t, rows, :] = kb
                vt_ref[0, t, 0:DIFF_VD, cols] = v_ref[src, :].astype(F32).T.astype(BF16)
                q2 = jnp.where(row == blk, _max_sq_norm(q_ref[src, :], map_selector), q2)
                k2 = jnp.where(lane == t, jnp.maximum(k2, _max_sq_norm(kb, map_selector)), k2)
            else:
                k_out_ref[0, t, rows, :] = jnp.zeros((BLOCK, hw), BF16)
                vt_ref[0, t, 0:DIFF_VD, cols] = jnp.zeros((DIFF_VD, BLOCK), BF16)
    first = _first_live_tile(q2, k2, pl.program_id(1).astype(F32), tk=tk, n_tiles=n_tiles)
    js_ref[0] = jnp.broadcast_to(first, (BLOCK, BLOCK)).astype(jnp.int32)


def _kvprep(proj, batch, p_rows, tk):
    n_real = p_rows // BLOCK
    n_tiles = -(-p_rows // tk)
    hw = 2 * DIFF_HD
    blocks_per_tile = tk // BLOCK
    assert hw == BLOCK and n_real <= BLOCK and blocks_per_tile & (blocks_per_tile - 1) == 0
    return pl.pallas_call(
        functools.partial(_kvprep_kernel, tk=tk, n_real=n_real, n_tiles=n_tiles),
        grid=(batch, DIFF_HEADS),
        in_specs=[
            pl.BlockSpec((p_rows, hw), lambda b, h: (b, COL_DQ // hw + h)),
            pl.BlockSpec((p_rows, hw), lambda b, h: (b, COL_DK // hw + h)),
            pl.BlockSpec((p_rows, DIFF_VD), lambda b, h: (b, COL_DV // DIFF_VD + h)),
        ],
        out_specs=[
            pl.BlockSpec((1, n_tiles, tk, hw), lambda b, h: (b * DIFF_HEADS + h, 0, 0, 0)),
            pl.BlockSpec((1, n_tiles, VT_ROWS, tk), lambda b, h: (b * DIFF_HEADS + h, 0, 0, 0)),
            pl.BlockSpec((1, BLOCK, BLOCK), lambda b, h: (b * DIFF_HEADS + h, 0, 0)),
        ],
        out_shape=[
            jax.ShapeDtypeStruct((batch * DIFF_HEADS, n_tiles, tk, hw), BF16),
            jax.ShapeDtypeStruct((batch * DIFF_HEADS, n_tiles, VT_ROWS, tk), BF16),
            jax.ShapeDtypeStruct((batch * DIFF_HEADS, BLOCK, BLOCK), jnp.int32),
        ],
        compiler_params=_params("parallel", "parallel"),
        name="kvprep",
    )(proj, proj, proj)


def _attn_kernel(first_ref, q_ref, k_ref, vt_ref, feat_ref, lamv_ref, lami_ref, sw_ref, o_ref,
                 s_ref, smax_ref, m_ref, acc_ref, *, tk, heads, q_blocks):
    tq = BLOCK
    hp = pl.program_id(1)
    lanes = 2 * tq
    hw = 2 * DIFF_HD
    blocks_per_tile = tk // BLOCK
    n_q_total = pl.num_programs(2) * q_blocks
    head0 = pl.program_id(0) * DIFF_HEADS + hp * heads

    lane_h = lax.broadcasted_iota(jnp.int32, (tq, hw), 1)
    row_q = lax.broadcasted_iota(jnp.int32, (tq, hw), 0).astype(F32)
    lv = lamv_ref[...]
    lam = (jnp.exp(jnp.sum(lv[0:1] * lv[1:2], axis=-1, keepdims=True))
           - jnp.exp(jnp.sum(lv[2:3] * lv[3:4], axis=-1, keepdims=True)) + lami_ref[...])

    def split_into(feat, first_lane, x):
        for piece in range(FEAT_SPLIT):
            head_part = x.astype(BF16).astype(F32)
            feat = jnp.where(lane_h == first_lane + piece, head_part, feat)
            x = x - head_part
        return feat

    def query_block(sub, carry):
        qi = pl.program_id(2) * q_blocks + sub
        q0 = qi * tq
        q_rows = pl.ds(pl.multiple_of(sub * tq, tq), tq)
        n_full = lax.shift_right_logical(qi, blocks_per_tile.bit_length() - 1)
        j_first = n_full
        for hh in range(heads):
            j_first = jnp.minimum(j_first, first_ref[(head0 + hh) * n_q_total + qi])

        def qcat_of(hh):
            head = (hp * heads + hh).astype(F32)
            slope2 = jnp.exp2(-(jnp.zeros((tq, hw), F32) + head + 1.0)) * LOG2E
            feat = jnp.where(lane_h == FEAT_MASK, NEG_INF, 0.0)
            feat = split_into(feat, FEAT_BLK, slope2 * BLOCK)
            feat = split_into(feat, FEAT_ROW, slope2)
            feat = split_into(feat, FEAT_Q0, -slope2 * q0.astype(F32))
            feat = split_into(feat, FEAT_QROW, -slope2 * row_q)
            q = q_ref[q_rows, hh * hw:(hh + 1) * hw].astype(F32)
            q_m0 = jnp.where(lane_h < DIFF_HD, q, 0.0)
            q_m1 = jnp.where(lane_h >= DIFF_HD, q, 0.0)
            top = jnp.concatenate([q_m0.T, q_m1.T], axis=1)
            bottom = jnp.concatenate([feat.T, feat.T], axis=1)
            return jnp.concatenate([top, bottom], axis=0).astype(BF16)

        qcats = [qcat_of(hh) for hh in range(heads)]

        def scores(j, slot):
            for hh in range(heads):
                k_aug = jnp.concatenate([k_ref[hh, j], feat_ref[j]], axis=1)
                s = jnp.dot(k_aug, qcats[hh], preferred_element_type=F32)
                s_ref[slot, hh] = s
                smax_ref[slot, hh] = jnp.max(s, axis=0, keepdims=True)

        def accumulate(hh, s, s_max, vt):
            m_old = m_ref[hh]
            m_new = jnp.maximum(m_old, s_max)
            alpha = jnp.exp2(m_old - m_new)
            p = jnp.exp2(s - m_new).astype(BF16)
            m_ref[hh] = m_new
            pv = jnp.dot(vt, p, preferred_element_type=F32)
            acc_ref[hh] = acc_ref[hh] * alpha + pv

        def update(j, slot):
            for hh in range(heads):
                accumulate(hh, s_ref[slot, hh], smax_ref[slot, hh], vt_ref[hh, j])

        def update_diagonal(j, slot, diag_block):
            live = (diag_block + 1) * BLOCK
            k_in_block = lax.broadcasted_iota(jnp.int32, (BLOCK, lanes), 0)
            q_in_block = lax.broadcasted_iota(jnp.int32, (BLOCK, lanes), 1)
            q_in_block = jnp.where(q_in_block >= tq, q_in_block - tq, q_in_block)
            for hh in range(heads):
                s = s_ref[slot, hh, 0:live, :]
                s_diag = jnp.where(k_in_block <= q_in_block, s[live - BLOCK:live], NEG_INF)
                if diag_block > 0:
                    s = jnp.concatenate([s[0:live - BLOCK], s_diag], axis=0)
                else:
                    s = s_diag
                accumulate(hh, s, jnp.max(s, axis=0, keepdims=True), vt_ref[hh, j, :, 0:live])
                o = acc_ref[hh, 0:DIFF_VD] / acc_ref[hh, DIFF_VD:DIFF_VD + 1]
                dlt = o[:, 0:tq] - lam * o[:, tq:lanes]
                ms = jnp.mean(dlt * dlt, axis=0, keepdims=True)
                y = dlt * lax.rsqrt(ms + EPS) * sw_ref[...] * (1.0 - lami_ref[...])
                o_ref[q_rows, hh * DIFF_VD:(hh + 1) * DIFF_VD] = y.T.astype(o_ref.dtype)

        m_ref[...] = jnp.full_like(m_ref, NEG_INF)
        acc_ref[...] = jnp.zeros_like(acc_ref)

        scores(j_first, 0)

        def tiles(j, count):
            for t in range(count):
                scores(j + t + 1, (t + 1) % 2)
                update(j + t, t % 2)

        def body4(i, c):
            tiles(j_first + 4 * i, 4)
            return c

        def body2(j, c):
            tiles(j, 2)
            return c

        def body1(j, c):
            tiles(j, 1)
            return c

        n_loop = n_full - j_first
        quads = lax.shift_right_logical(n_loop, 2)
        j_pair = j_first + 4 * quads
        j_single = j_pair + (n_loop & 2)
        lax.fori_loop(0, quads, body4, 0)
        lax.fori_loop(j_pair, j_pair + lax.shift_right_logical(n_loop & 2, 1), body2, 0)
        lax.fori_loop(j_single, n_full, body1, 0)
        diag_block = qi & (blocks_per_tile - 1)
        diag_slot = n_loop & 1
        for c in range(blocks_per_tile):
            pl.when(diag_block == c)(functools.partial(update_diagonal, n_full, diag_slot, c))
        return carry

    lax.fori_loop(0, q_blocks, query_block, 0)


def _position_features(n_tiles, tk, p_rows):
    kpos = jnp.arange(n_tiles * tk, dtype=jnp.int32)[:, None]
    lane = jnp.arange(2 * DIFF_HD, dtype=jnp.int32)[None, :]
    masked = (kpos < META_PAD) | (kpos >= p_rows)
    feat = jnp.where(lane < FEAT_ROW, kpos // BLOCK,
           jnp.where(lane < FEAT_MASK, kpos % BLOCK,
           jnp.where(lane == FEAT_MASK, masked.astype(jnp.int32),
           jnp.where(lane < FEAT_QROW + FEAT_SPLIT, 1, 0))))
    return feat.astype(BF16).reshape(n_tiles, tk, 2 * DIFF_HD)


def _diff_attention(proj, lam_vecs, lam_init, subln_w, batch, p_rows):
    n = proj.shape[0]
    tq = BLOCK
    tk = ATTN_KEY_TILE
    heads = ATTN_HEADS_PER_STEP
    nq = p_rows // tq
    q_blocks = _largest_divisor(nq, (5, 3, 1))
    steps = nq // q_blocks
    tr = q_blocks * tq
    hw = 2 * DIFF_HD
    k_tiles, v_t, first_tile = _kvprep(proj, batch, p_rows, tk)
    first_tile = first_tile[:, :nq, 0].reshape(batch * DIFF_HEADS * nq)
    n_tiles = k_tiles.shape[1]
    feat = _position_features(n_tiles, tk, p_rows)
    groups = DIFF_HEADS // heads
    grid_spec = pltpu.PrefetchScalarGridSpec(
        num_scalar_prefetch=1,
        grid=(batch, groups, steps),
        in_specs=[
            pl.BlockSpec((tr, heads * hw), lambda b, g, i, first: (b * steps + i, COL_DQ // (heads * hw) + g)),
            pl.BlockSpec((heads, n_tiles, tk, hw), lambda b, g, i, first: (b * groups + g, 0, 0, 0)),
            pl.BlockSpec((heads, n_tiles, VT_ROWS, tk), lambda b, g, i, first: (b * groups + g, 0, 0, 0)),
            pl.BlockSpec((n_tiles, tk, hw), lambda b, g, i, first: (0, 0, 0), pipeline_mode=pl.Buffered(1)),
            pl.BlockSpec((4, DIFF_HD), lambda b, g, i, first: (0, 0)),
            pl.BlockSpec((1, 1), lambda b, g, i, first: (0, 0)),
            pl.BlockSpec((DIFF_VD, 1), lambda b, g, i, first: (0, 0)),
        ],
        out_specs=pl.BlockSpec((tr, heads * DIFF_VD), lambda b, g, i, first: (b * steps + i, g)),
        scratch_shapes=[
            pltpu.VMEM((2, heads, tk, 2 * tq), F32),
            pltpu.VMEM((2, heads, 1, 2 * tq), F32),
            pltpu.VMEM((heads, 1, 2 * tq), F32),
            pltpu.VMEM((heads, VT_ROWS, 2 * tq), F32),
        ],
    )
    return pl.pallas_call(
        functools.partial(_attn_kernel, tk=tk, heads=heads, q_blocks=q_blocks),
        grid_spec=grid_spec,
        out_shape=jax.ShapeDtypeStruct((n, D_MODEL), BF16),
        compiler_params=_params("parallel", "parallel", "arbitrary"),
        name="diffattn",
    )(first_tile, proj, k_tiles, v_t, feat, lam_vecs, lam_init, subln_w)


def _merge_kernel(ret_ref, diff_ref, cb_ref, cc_ref, cx_ref, cch_ref, cxh_ref, g_ref, x_ref,
                  cw_ref, wb_ref, wo_ref, o_ref, *, tm, p_rows):
    start = pl.program_id(0) * tm
    rowi = lax.broadcasted_iota(jnp.int32, (tm, D_MODEL), 0)

    u = cc_ref[...].astype(F32) * cx_ref[...].astype(F32)
    uh = cch_ref[...].astype(F32) * cxh_ref[...].astype(F32)
    h1 = uh[BF16_SUBLANES - 1:BF16_SUBLANES, :]
    h2 = uh[BF16_SUBLANES - 2:BF16_SUBLANES - 1, :]
    u1 = jnp.where(rowi == 0, h1, pltpu.roll(u, 1, axis=0))
    u2 = jnp.where(rowi == 0, h2, jnp.where(rowi == 1, h1, pltpu.roll(u, 2, axis=0)))
    cw = cw_ref[...]
    conv = cb_ref[...].astype(F32) * (cw[0:1] * u2 + cw[1:2] * u1 + cw[2:3] * u)

    g = g_ref[...].astype(F32)
    g = 1.0 / (1.0 + jnp.exp(-g))
    merged = (g[:, 0:D_MODEL] * jnp.dot(ret_ref[...], wb_ref[0], preferred_element_type=F32)
              + g[:, D_MODEL:2 * D_MODEL] * jnp.dot(diff_ref[...], wb_ref[1], preferred_element_type=F32)
              + g[:, 2 * D_MODEL:] * jnp.dot(conv.astype(BF16), wb_ref[2], preferred_element_type=F32))
    y = x_ref[...] + jnp.dot(merged.astype(BF16), wo_ref[...], preferred_element_type=F32)
    r = rowi + (start - (start // p_rows) * p_rows)
    is_pad = (r < META_PAD) | ((r >= p_rows) & (r < p_rows + META_PAD))
    o_ref[...] = jnp.where(is_pad, 0.0, y)


def _merge(ret, diff, proj, x, conv_w, w_branch, w_out, p_rows):
    n, d = x.shape
    tm = _largest_divisor(n, (512, 256, 128))
    halo_blocks = tm // BF16_SUBLANES

    def col(c):
        return lambda i: (i, c // d)

    def halo(c):
        return lambda i: (jnp.maximum(i * halo_blocks - 1, 0), c // d)

    return pl.pallas_call(
        functools.partial(_merge_kernel, tm=tm, p_rows=p_rows),
        grid=(n // tm,),
        in_specs=[
            pl.BlockSpec((tm, d), lambda i: (i, 0)),
            pl.BlockSpec((tm, d), lambda i: (i, 0)),
            pl.BlockSpec((tm, d), col(COL_CB)),
            pl.BlockSpec((tm, d), col(COL_CC)),
            pl.BlockSpec((tm, d), col(COL_CX)),
            pl.BlockSpec((BF16_SUBLANES, d), halo(COL_CC)),
            pl.BlockSpec((BF16_SUBLANES, d), halo(COL_CX)),
            pl.BlockSpec((tm, N_BRANCH * d), lambda i: (i, COL_GATE // (N_BRANCH * d))),
            pl.BlockSpec((tm, d), lambda i: (i, 0)),
            pl.BlockSpec((CONV_K, d), lambda i: (0, 0)),
            pl.BlockSpec((N_BRANCH, d, d), lambda i: (0, 0, 0), pipeline_mode=pl.Buffered(1)),
            pl.BlockSpec((d, d), lambda i: (0, 0), pipeline_mode=pl.Buffered(1)),
        ],
        out_specs=pl.BlockSpec((tm, d), lambda i: (i, 0)),
        out_shape=jax.ShapeDtypeStruct((n, d), F32),
        compiler_params=_params("parallel"),
        name="merge",
    )(ret, diff, proj, proj, proj, proj, proj, proj, x, conv_w, w_branch, w_out)


def _mlp_kernel(x_ref, nw_ref, wu_ref, wd_ref, o_ref, *, ff_chunk):
    x = x_ref[...]
    xn = _rms(x, nw_ref[...]).astype(BF16)
    acc = x
    for c in range(wu_ref.shape[1] // ff_chunk):
        cols = slice(c * ff_chunk, (c + 1) * ff_chunk)
        u = jnp.maximum(jnp.dot(xn, wu_ref[:, cols], preferred_element_type=F32), 0.0)
        acc = acc + jnp.dot((u * u).astype(BF16), wd_ref[cols, :], preferred_element_type=F32)
    o_ref[...] = acc


def _mlp(x, norm_w, w_up, w_down):
    n, d = x.shape
    d_ff = w_up.shape[1]
    tm = _largest_divisor(n, (512, 256, 128))
    return pl.pallas_call(
        functools.partial(_mlp_kernel, ff_chunk=1024),
        grid=(n // tm,),
        in_specs=[
            pl.BlockSpec((tm, d), lambda i: (i, 0)),
            pl.BlockSpec((1, d), lambda i: (0, 0)),
            pl.BlockSpec((d, d_ff), lambda i: (0, 0)),
            pl.BlockSpec((d_ff, d), lambda i: (0, 0)),
        ],
        out_specs=pl.BlockSpec((tm, d), lambda i: (i, 0)),
        out_shape=jax.ShapeDtypeStruct((n, d), F32),
        compiler_params=_params("parallel"),
        name="mlp",
    )(x, norm_w, w_up, w_down)


def _final_norm_kernel(x_ref, w_ref, o_ref):
    o_ref[...] = _rms(x_ref[...], w_ref[...])


def _final_norm(h, w, batch, seq, p_rows):
    d = h.shape[1]
    tr = _largest_divisor(seq, (1024, 512, 256, 128))
    steps = seq // tr
    out = pl.pallas_call(
        _final_norm_kernel,
        grid=(batch, steps),
        in_specs=[
            pl.BlockSpec((pl.Element(tr), pl.Element(d)),
                         lambda b, i: (pl.multiple_of(b * p_rows + BLOCK + i * tr, BLOCK), 0)),
            pl.BlockSpec((1, d), lambda b, i: (0, 0)),
        ],
        out_specs=pl.BlockSpec((tr, d), lambda b, i: (b * steps + i, 0)),
        out_shape=jax.ShapeDtypeStruct((batch * seq, d), h.dtype),
        compiler_params=_params("parallel", "parallel"),
        name="final_norm",
    )(h, w)
    return out.reshape(batch, seq, d)


def kernel(x, meta_tokens, norm1_w, w_in, conv_w, diff_lambda, diff_subln_w, w_branch, w_out,
           norm2_w, w_up, w_down, final_norm_w):
    batch, seq, d = x.shape
    depth = w_in.shape[0]
    assert d == D_MODEL and w_in.shape[1:] == (D_MODEL, D_IN) and seq % BLOCK == 0
    p_rows = META_PAD + N_META + seq

    meta = jnp.broadcast_to(meta_tokens.astype(x.dtype)[None], (batch, N_META, d))
    h = jnp.concatenate([jnp.zeros((batch, META_PAD, d), x.dtype), meta, x], axis=1).reshape(batch * p_rows, d)

    layer_ids = jnp.arange(depth, dtype=F32)
    lam_inits = (0.8 - 0.6 * jnp.exp(-0.3 * layer_ids)).reshape(depth, 1, 1)
    w_in = w_in.at[:, :, COL_DQ:COL_DK].multiply(DIFF_HD ** -0.5 * LOG2E)
    layers = dict(
        norm1_w=norm1_w.reshape(depth, 1, d), w_in=w_in.astype(BF16), conv_w=conv_w,
        lam_vecs=diff_lambda.astype(F32), lam_init=lam_inits, subln_w=diff_subln_w.reshape(depth, DIFF_VD, 1),
        w_branch=w_branch.astype(BF16), w_out=w_out.astype(BF16), norm2_w=norm2_w.reshape(depth, 1, d),
        w_up=w_up.astype(BF16), w_down=w_down.astype(BF16))

    def layer(h, p):
        proj = _inproj(h, p["norm1_w"], p["w_in"])
        ret = _retention(proj, batch, p_rows)
        diff = _diff_attention(proj, p["lam_vecs"], p["lam_init"], p["subln_w"], batch, p_rows)
        h = _merge(ret, diff, proj, h, p["conv_w"], p["w_branch"], p["w_out"], p_rows)
        h = _mlp(h, p["norm2_w"], p["w_up"], p["w_down"])
        return h, None

    h, _ = lax.scan(layer, h, layers)
    return _final_norm(h, final_norm_w.reshape(1, d), batch, seq, p_rows)
```

```python
import functools
import math

import jax
import jax.numpy as jnp
from jax import lax
from jax.experimental import pallas as pl
from jax.experimental.pallas import tpu as pltpu

N_META = 16
BLOCK = 128
META_PAD = BLOCK - N_META
RET_HEADS = 4
RET_DK = 128
RET_DV = 256
DIFF_HEADS = 8
DIFF_HD = 64
DIFF_VD = 128
CONV_K = 3
N_BRANCH = 3
EPS = 1e-6
NEG_INF = -1e30

D_MODEL = RET_HEADS * RET_DV
COL_RQ, COL_RK, COL_RV, COL_RG = 0, 512, 1024, 2048
COL_DQ, COL_DK, COL_DV = 3072, 4096, 5120
COL_CB, COL_CC, COL_CX, COL_GATE = 6144, 7168, 8192, 9216
D_IN = COL_GATE + N_BRANCH * D_MODEL

V7X_VMEM_LIMIT_BYTES = 56 * 1024 * 1024
BF16_SUBLANES = 16
ATTN_KEY_TILE = 512
ATTN_MAX_TRIP_TILES = 4
ATTN_HEADS_PER_STEP = 4

F32 = jnp.float32
BF16 = jnp.bfloat16


def _largest_divisor(n, candidates):
    for c in candidates:
        if n % c == 0:
            return c
    raise ValueError(f"no tile in {candidates} divides {n}")


def _params(*semantics):
    return pltpu.CompilerParams(dimension_semantics=semantics, vmem_limit_bytes=V7X_VMEM_LIMIT_BYTES)


def _rms(x, w):
    return x * lax.rsqrt(jnp.mean(x * x, axis=-1, keepdims=True) + EPS) * w


def _inproj_kernel(x_ref, nw_ref, w_ref, o_ref, xn_ref):
    @pl.when(pl.program_id(1) == 0)
    def _():
        xn_ref[...] = _rms(x_ref[...], nw_ref[...]).astype(BF16)

    o_ref[...] = jnp.dot(xn_ref[...], w_ref[...], preferred_element_type=F32).astype(o_ref.dtype)


def _inproj(h, norm_w, w_in):
    n, d = h.shape
    tm = _largest_divisor(n, (1024, 512, 256, 128))
    tn = 2048
    return pl.pallas_call(
        _inproj_kernel,
        grid=(n // tm, D_IN // tn),
        in_specs=[
            pl.BlockSpec((tm, d), lambda i, j: (i, 0)),
            pl.BlockSpec((1, d), lambda i, j: (0, 0)),
            pl.BlockSpec((d, tn), lambda i, j: (0, j)),
        ],
        out_specs=pl.BlockSpec((tm, tn), lambda i, j: (i, j)),
        out_shape=jax.ShapeDtypeStruct((n, D_IN), BF16),
        scratch_shapes=[pltpu.VMEM((tm, d), BF16)],
        compiler_params=_params("parallel", "arbitrary"),
        name="inproj",
    )(h, norm_w, w_in)


def _ret_log_gamma(head):
    return math.log1p(-(2.0 ** (-5.0 - head)))


def _retention_kernel(q_ref, k_ref, v_ref, g_ref, o_ref, state_ref, intra_ref, qd_ref, kd_ref, *, chunks):
    @pl.when(pl.program_id(1) == 0)
    def _():
        state_ref[...] = jnp.zeros_like(state_ref)
        scale = RET_DK ** -0.5
        i_sq = lax.broadcasted_iota(jnp.int32, (BLOCK, BLOCK), 0).astype(F32)
        j_sq = lax.broadcasted_iota(jnp.int32, (BLOCK, BLOCK), 1).astype(F32)
        i_dv = lax.broadcasted_iota(jnp.int32, (BLOCK, RET_DV), 0).astype(F32)
        dist = i_sq - j_sq
        for h in range(RET_HEADS):
            lg = _ret_log_gamma(h)
            intra_ref[h] = jnp.where(dist >= 0, jnp.exp(lg * jnp.maximum(dist, 0.0)), 0.0) * scale
            qd_ref[h] = jnp.exp(lg * (i_dv + 1.0))
            kd_ref[h] = jnp.exp(lg * (BLOCK - 1.0 - i_sq)) * scale

    for c in range(chunks):
        rows = pl.ds(c * BLOCK, BLOCK)
        for h in range(RET_HEADS):
            s_decay = math.exp(_ret_log_gamma(h) * BLOCK)
            q = q_ref[rows, h * RET_DK:(h + 1) * RET_DK]
            k = k_ref[rows, h * RET_DK:(h + 1) * RET_DK]
            v = v_ref[rows, h * RET_DV:(h + 1) * RET_DV]
            st = state_ref[h]
            scores = lax.dot_general(q, k, (((1,), (1,)), ((), ())), preferred_element_type=F32)
            scores = (scores * intra_ref[h]).astype(BF16)
            out = (jnp.dot(scores, v, preferred_element_type=F32)
                   + qd_ref[h] * jnp.dot(q, st.astype(BF16), preferred_element_type=F32))
            kdec = (k.astype(F32) * kd_ref[h]).astype(BF16)
            state_ref[h] = st * s_decay + lax.dot_general(
                kdec, v, (((0,), (0,)), ((), ())), preferred_element_type=F32)
            mu = jnp.mean(out, axis=-1, keepdims=True)
            cen = out - mu
            var = jnp.mean(cen * cen, axis=-1, keepdims=True)
            g = g_ref[rows, h * RET_DV:(h + 1) * RET_DV].astype(F32)
            silu = g / (1.0 + jnp.exp(-g))
            o_ref[rows, h * RET_DV:(h + 1) * RET_DV] = (cen * lax.rsqrt(var + EPS) * silu).astype(o_ref.dtype)


def _retention(proj, batch, p_rows):
    n = proj.shape[0]
    nc = p_rows // BLOCK
    chunks = _largest_divisor(nc, (5, 3, 1))
    steps = nc // chunks
    tr = chunks * BLOCK
    qk_w = RET_HEADS * RET_DK

    def rows(b, c):
        return b * steps + c

    return pl.pallas_call(
        functools.partial(_retention_kernel, chunks=chunks),
        grid=(batch, steps),
        in_specs=[
            pl.BlockSpec((tr, qk_w), lambda b, c: (rows(b, c), COL_RQ // qk_w)),
            pl.BlockSpec((tr, qk_w), lambda b, c: (rows(b, c), COL_RK // qk_w)),
            pl.BlockSpec((tr, D_MODEL), lambda b, c: (rows(b, c), COL_RV // D_MODEL)),
            pl.BlockSpec((tr, D_MODEL), lambda b, c: (rows(b, c), COL_RG // D_MODEL)),
        ],
        out_specs=pl.BlockSpec((tr, D_MODEL), lambda b, c: (rows(b, c), 0)),
        out_shape=jax.ShapeDtypeStruct((n, D_MODEL), BF16),
        scratch_shapes=[
            pltpu.VMEM((RET_HEADS, RET_DK, RET_DV), F32),
            pltpu.VMEM((RET_HEADS, BLOCK, BLOCK), F32),
            pltpu.VMEM((RET_HEADS, BLOCK, RET_DV), F32),
            pltpu.VMEM((RET_HEADS, BLOCK, BLOCK), F32),
        ],
        compiler_params=_params("parallel", "arbitrary"),
        name="retention",
    )(proj, proj, proj, proj)


FEAT_BLK, FEAT_ROW, FEAT_MASK, FEAT_Q0, FEAT_QROW = 0, 3, 6, 7, 10
FEAT_SPLIT = 3
LOG2E = math.log2(math.e)
VT_ROWS = DIFF_VD + BF16_SUBLANES
SKIP_LOG2_MARGIN = 160.0
SKIP_NORM_SLACK = 1.01


def _max_sq_norm(x, map_selector):
    sq = x.astype(F32)
    sq = (sq * sq).astype(BF16)
    per_map = jnp.dot(sq, map_selector, preferred_element_type=F32)
    return jnp.max(jnp.max(per_map, axis=0, keepdims=True), axis=-1, keepdims=True)


def _first_live_tile(q2, k2, head, *, tk, n_tiles):
    blocks_per_tile = tk // BLOCK
    qi = lax.broadcasted_iota(jnp.int32, (BLOCK, BLOCK), 0)
    tj = lax.broadcasted_iota(jnp.int32, (BLOCK, BLOCK), 1)
    diag_tile = lax.shift_right_logical(qi, blocks_per_tile.bit_length() - 1)
    slope2 = jnp.exp2(-(jnp.zeros((BLOCK, BLOCK), F32) + head + 1.0)) * LOG2E
    kd2 = jnp.max(jnp.where(tj == diag_tile, k2, 0.0), axis=-1, keepdims=True)
    dist_min = (qi * BLOCK - tj * tk - (tk - 1)).astype(F32)
    bound = SKIP_NORM_SLACK * jnp.sqrt(q2) * (jnp.sqrt(k2) + jnp.sqrt(kd2)) + SKIP_LOG2_MARGIN
    dead = jnp.where((slope2 * dist_min > bound) & (tj < diag_tile), 1.0, 0.0)
    run = jnp.ones((BLOCK, 1), F32)
    count = jnp.zeros((BLOCK, 1), F32)
    for j in range(n_tiles):
        run = run * dead[:, j:j + 1]
        count = count + run
    return count


def _kvprep_kernel(q_ref, k_ref, v_ref, k_out_ref, vt_ref, js_ref, *, tk, n_real, n_tiles):
    blocks_per_tile = tk // BLOCK
    hw = 2 * DIFF_HD
    lane = lax.broadcasted_iota(jnp.int32, (BLOCK, hw), 1)
    row = lax.broadcasted_iota(jnp.int32, (BLOCK, hw), 0)
    map_selector = jnp.where(((lane == 0) & (row < DIFF_HD)) | ((lane == 1) & (row >= DIFF_HD)), 1.0, 0.0)
    map_selector = map_selector.astype(BF16)
    ones_row = jnp.where(lax.broadcasted_iota(jnp.int32, (VT_ROWS - DIFF_VD, BLOCK), 0) == 0, 1.0, 0.0)
    q2 = jnp.zeros((BLOCK, BLOCK), F32)
    k2 = jnp.zeros((BLOCK, BLOCK), F32)
    for t in range(n_tiles):
        for bi in range(blocks_per_tile):
            blk = t * blocks_per_tile + bi
            rows = pl.ds(bi * BLOCK, BLOCK)
            cols = pl.ds(bi * BLOCK, BLOCK)
            vt_ref[0, t, DIFF_VD:, cols] = ones_row.astype(BF16)
            if blk < n_real:
                src = slice(blk * BLOCK, (blk + 1) * BLOCK)
                kb = k_ref[src, :]
                k_out_ref[0, t, rows, :] = kb
                vt_ref[0, t, 0:DIFF_VD, cols] = v_ref[src, :].astype(F32).T.astype(BF16)
                q2 = jnp.where(row == blk, _max_sq_norm(q_ref[src, :], map_selector), q2)
                k2 = jnp.where(lane == t, jnp.maximum(k2, _max_sq_norm(kb, map_selector)), k2)
            else:
                k_out_ref[0, t, rows, :] = jnp.zeros((BLOCK, hw), BF16)
                vt_ref[0, t, 0:DIFF_VD, cols] = jnp.zeros((DIFF_VD, BLOCK), BF16)
    first = _first_live_tile(q2, k2, pl.program_id(1).astype(F32), tk=tk, n_tiles=n_tiles)
    js_ref[0] = jnp.broadcast_to(first, (BLOCK, BLOCK)).astype(jnp.int32)


def _kvprep(proj, batch, p_rows, tk):
    n_real = p_rows // BLOCK
    n_tiles = -(-p_rows // tk)
    hw = 2 * DIFF_HD
    blocks_per_tile = tk // BLOCK
    assert hw == BLOCK and n_real <= BLOCK and blocks_per_tile & (blocks_per_tile - 1) == 0
    return pl.pallas_call(
        functools.partial(_kvprep_kernel, tk=tk, n_real=n_real, n_tiles=n_tiles),
        grid=(batch, DIFF_HEADS),
        in_specs=[
            pl.BlockSpec((p_rows, hw), lambda b, h: (b, COL_DQ // hw + h)),
            pl.BlockSpec((p_rows, hw), lambda b, h: (b, COL_DK // hw + h)),
            pl.BlockSpec((p_rows, DIFF_VD), lambda b, h: (b, COL_DV // DIFF_VD + h)),
        ],
        out_specs=[
            pl.BlockSpec((1, n_tiles, tk, hw), lambda b, h: (b * DIFF_HEADS + h, 0, 0, 0)),
            pl.BlockSpec((1, n_tiles, VT_ROWS, tk), lambda b, h: (b * DIFF_HEADS + h, 0, 0, 0)),
            pl.BlockSpec((1, BLOCK, BLOCK), lambda b, h: (b * DIFF_HEADS + h, 0, 0)),
        ],
        out_shape=[
            jax.ShapeDtypeStruct((batch * DIFF_HEADS, n_tiles, tk, hw), BF16),
            jax.ShapeDtypeStruct((batch * DIFF_HEADS, n_tiles, VT_ROWS, tk), BF16),
            jax.ShapeDtypeStruct((batch * DIFF_HEADS, BLOCK, BLOCK), jnp.int32),
        ],
        compiler_params=_params("parallel", "parallel"),
        name="kvprep",
    )(proj, proj, proj)


def _attn_kernel(first_ref, q_ref, qn_ref, k_ref, vt_ref, feat_ref, lamv_ref, lami_ref, sw_ref, o_ref,
                 s_ref, smax_ref, m_ref, acc_ref, qcat_ref, *, tk, heads, q_blocks):
    tq = BLOCK
    hp = pl.program_id(1)
    lanes = 2 * tq
    hw = 2 * DIFF_HD
    blocks_per_tile = tk // BLOCK
    n_q_total = pl.num_programs(2) * q_blocks
    head0 = pl.program_id(0) * DIFF_HEADS + hp * heads

    lane_h = lax.broadcasted_iota(jnp.int32, (tq, hw), 1)
    row_q = lax.broadcasted_iota(jnp.int32, (tq, hw), 0).astype(F32)
    lv = lamv_ref[...]
    lam = (jnp.exp(jnp.sum(lv[0:1] * lv[1:2], axis=-1, keepdims=True))
           - jnp.exp(jnp.sum(lv[2:3] * lv[3:4], axis=-1, keepdims=True)) + lami_ref[...])

    def split_into(feat, first_lane, x):
        for piece in range(FEAT_SPLIT):
            head_part = x.astype(BF16).astype(F32)
            feat = jnp.where(lane_h == first_lane + piece, head_part, feat)
            x = x - head_part
        return feat

    def tile_range(qi):
        n_full = lax.shift_right_logical(qi, blocks_per_tile.bit_length() - 1)
        j_first = n_full
        for hh in range(heads):
            j_first = jnp.minimum(j_first, first_ref[(head0 + hh) * n_q_total + qi])
        return n_full, j_first

    def qcat_of(hh, qi, q_block):
        head = (hp * heads + hh).astype(F32)
        slope2 = jnp.exp2(-(jnp.zeros((tq, hw), F32) + head + 1.0)) * LOG2E
        feat = jnp.where(lane_h == FEAT_MASK, NEG_INF, 0.0)
        feat = split_into(feat, FEAT_BLK, slope2 * BLOCK)
        feat = split_into(feat, FEAT_ROW, slope2)
        feat = split_into(feat, FEAT_Q0, -slope2 * (qi * tq).astype(F32))
        feat = split_into(feat, FEAT_QROW, -slope2 * row_q)
        q = q_block[:, hh * hw:(hh + 1) * hw].astype(F32)
        q_m0 = jnp.where(lane_h < DIFF_HD, q, 0.0)
        q_m1 = jnp.where(lane_h >= DIFF_HD, q, 0.0)
        top = jnp.concatenate([q_m0.T, q_m1.T], axis=1)
        bottom = jnp.concatenate([feat.T, feat.T], axis=1)
        return jnp.concatenate([top, bottom], axis=0).astype(BF16)

    def score_tile(j, slot, qcats):
        for hh in range(heads):
            k_aug = jnp.concatenate([k_ref[hh, j], feat_ref[j]], axis=1)
            s = jnp.dot(k_aug, qcats[hh], preferred_element_type=F32)
            s_ref[slot, hh] = s
            smax_ref[slot, hh] = jnp.max(s, axis=0, keepdims=True)

    def start_block(qi, q_block):
        _, j_first = tile_range(qi)
        qcats = [qcat_of(hh, qi, q_block) for hh in range(heads)]
        for hh in range(heads):
            qcat_ref[hh] = qcats[hh]
        score_tile(j_first, 0, qcats)

    @pl.when(pl.program_id(2) == 0)
    def _():
        start_block(pl.program_id(2) * q_blocks, q_ref[0:tq, :])

    def query_block(sub, carry):
        qi = pl.program_id(2) * q_blocks + sub
        q_rows = pl.ds(pl.multiple_of(sub * tq, tq), tq)
        n_full, j_first = tile_range(qi)
        qcats = [qcat_ref[hh] for hh in range(heads)]

        def scores(j, slot):
            score_tile(j, slot, qcats)

        def accumulate(hh, s, s_max, vt):
            m_old = m_ref[hh]
            m_new = jnp.maximum(m_old, s_max)
            alpha = jnp.exp2(m_old - m_new)
            p = jnp.exp2(s - m_new).astype(BF16)
            m_ref[hh] = m_new
            pv = jnp.dot(vt, p, preferred_element_type=F32)
            acc_ref[hh] = acc_ref[hh] * alpha + pv

        def update(j, slot):
            for hh in range(heads):
                accumulate(hh, s_ref[slot, hh], smax_ref[slot, hh], vt_ref[hh, j])

        def update_diagonal(j, slot, diag_block):
            live = (diag_block + 1) * BLOCK
            k_in_block = lax.broadcasted_iota(jnp.int32, (BLOCK, lanes), 0)
            q_in_block = lax.broadcasted_iota(jnp.int32, (BLOCK, lanes), 1)
            q_in_block = jnp.where(q_in_block >= tq, q_in_block - tq, q_in_block)
            for hh in range(heads):
                s = s_ref[slot, hh, 0:live, :]
                s_diag = jnp.where(k_in_block <= q_in_block, s[live - BLOCK:live], NEG_INF)
                if diag_block > 0:
                    s = jnp.concatenate([s[0:live - BLOCK], s_diag], axis=0)
                else:
                    s = s_diag
                accumulate(hh, s, jnp.max(s, axis=0, keepdims=True), vt_ref[hh, j, :, 0:live])
                o = acc_ref[hh, 0:DIFF_VD] / acc_ref[hh, DIFF_VD:DIFF_VD + 1]
                dlt = o[:, 0:tq] - lam * o[:, tq:lanes]
                ms = jnp.mean(dlt * dlt, axis=0, keepdims=True)
                y = dlt * lax.rsqrt(ms + EPS) * sw_ref[...] * (1.0 - lami_ref[...])
                o_ref[q_rows, hh * DIFF_VD:(hh + 1) * DIFF_VD] = y.T.astype(o_ref.dtype)
            next_rows = pl.ds(pl.multiple_of(jnp.minimum(sub + 1, q_blocks - 1) * tq, tq), tq)
            q_next = jnp.where(sub == q_blocks - 1, qn_ref[...], q_ref[next_rows, :])
            start_block(jnp.minimum(qi + 1, n_q_total - 1), q_next)

        m_ref[...] = jnp.full_like(m_ref, NEG_INF)
        acc_ref[...] = jnp.zeros_like(acc_ref)


        def tiles(j, count):
            for t in range(count):
                scores(j + t + 1, (t + 1) % 2)
                update(j + t, t % 2)

        n_loop = n_full - j_first
        j = j_first
        count = ATTN_MAX_TRIP_TILES
        while count >= 1:
            def body(i, c, base=j, count=count):
                tiles(base + count * i, count)
                return c
            if count == ATTN_MAX_TRIP_TILES:
                trips = lax.shift_right_logical(n_loop, count.bit_length() - 1)
            else:
                trips = lax.shift_right_logical(n_loop & count, count.bit_length() - 1)
            lax.fori_loop(0, trips, body, 0)
            j = j + count * trips
            count //= 2
        diag_block = qi & (blocks_per_tile - 1)
        diag_slot = n_loop & 1
        for c in range(blocks_per_tile):
            pl.when(diag_block == c)(functools.partial(update_diagonal, n_full, diag_slot, c))
        return carry

    lax.fori_loop(0, q_blocks, query_block, 0)


def _position_features(n_tiles, tk, p_rows):
    kpos = jnp.arange(n_tiles * tk, dtype=jnp.int32)[:, None]
    lane = jnp.arange(2 * DIFF_HD, dtype=jnp.int32)[None, :]
    masked = (kpos < META_PAD) | (kpos >= p_rows)
    feat = jnp.where(lane < FEAT_ROW, kpos // BLOCK,
           jnp.where(lane < FEAT_MASK, kpos % BLOCK,
           jnp.where(lane == FEAT_MASK, masked.astype(jnp.int32),
           jnp.where(lane < FEAT_QROW + FEAT_SPLIT, 1, 0))))
    return feat.astype(BF16).reshape(n_tiles, tk, 2 * DIFF_HD)


def _diff_attention(proj, lam_vecs, lam_init, subln_w, batch, p_rows):
    n = proj.shape[0]
    tq = BLOCK
    tk = ATTN_KEY_TILE
    heads = ATTN_HEADS_PER_STEP
    nq = p_rows // tq
    q_blocks = _largest_divisor(nq, (5, 3, 1))
    steps = nq // q_blocks
    tr = q_blocks * tq
    hw = 2 * DIFF_HD
    k_tiles, v_t, first_tile = _kvprep(proj, batch, p_rows, tk)
    first_tile = first_tile[:, :nq, 0].reshape(batch * DIFF_HEADS * nq)
    n_tiles = k_tiles.shape[1]
    feat = _position_features(n_tiles, tk, p_rows)
    groups = DIFF_HEADS // heads
    grid_spec = pltpu.PrefetchScalarGridSpec(
        num_scalar_prefetch=1,
        grid=(batch, groups, steps),
        in_specs=[
            pl.BlockSpec((tr, heads * hw), lambda b, g, i, first: (b * steps + i, COL_DQ // (heads * hw) + g)),
            pl.BlockSpec((tq, heads * hw), lambda b, g, i, first: (
                b * nq + jnp.minimum((i + 1) * q_blocks, nq - 1), COL_DQ // (heads * hw) + g)),
            pl.BlockSpec((heads, n_tiles, tk, hw), lambda b, g, i, first: (b * groups + g, 0, 0, 0)),
            pl.BlockSpec((heads, n_tiles, VT_ROWS, tk), lambda b, g, i, first: (b * groups + g, 0, 0, 0)),
            pl.BlockSpec((n_tiles, tk, hw), lambda b, g, i, first: (0, 0, 0), pipeline_mode=pl.Buffered(1)),
            pl.BlockSpec((4, DIFF_HD), lambda b, g, i, first: (0, 0)),
            pl.BlockSpec((1, 1), lambda b, g, i, first: (0, 0)),
            pl.BlockSpec((DIFF_VD, 1), lambda b, g, i, first: (0, 0)),
        ],
        out_specs=pl.BlockSpec((tr, heads * DIFF_VD), lambda b, g, i, first: (b * steps + i, g)),
        scratch_shapes=[
            pltpu.VMEM((2, heads, tk, 2 * tq), F32),
            pltpu.VMEM((2, heads, 1, 2 * tq), F32),
            pltpu.VMEM((heads, 1, 2 * tq), F32),
            pltpu.VMEM((heads, VT_ROWS, 2 * tq), F32),
            pltpu.VMEM((heads, 2 * hw, 2 * tq), BF16),
        ],
    )
    return pl.pallas_call(
        functools.partial(_attn_kernel, tk=tk, heads=heads, q_blocks=q_blocks),
        grid_spec=grid_spec,
        out_shape=jax.ShapeDtypeStruct((n, D_MODEL), BF16),
        compiler_params=_params("arbitrary", "arbitrary", "arbitrary"),
        name="diffattn",
    )(first_tile, proj, proj, k_tiles, v_t, feat, lam_vecs, lam_init, subln_w)


def _merge_kernel(ret_ref, diff_ref, cb_ref, cc_ref, cx_ref, cch_ref, cxh_ref, g_ref, x_ref,
                  cw_ref, wb_ref, wo_ref, o_ref, *, tm, p_rows):
    start = pl.program_id(0) * tm
    rowi = lax.broadcasted_iota(jnp.int32, (tm, D_MODEL), 0)

    u = cc_ref[...].astype(F32) * cx_ref[...].astype(F32)
    uh = cch_ref[...].astype(F32) * cxh_ref[...].astype(F32)
    h1 = uh[BF16_SUBLANES - 1:BF16_SUBLANES, :]
    h2 = uh[BF16_SUBLANES - 2:BF16_SUBLANES - 1, :]
    u1 = jnp.where(rowi == 0, h1, pltpu.roll(u, 1, axis=0))
    u2 = jnp.where(rowi == 0, h2, jnp.where(rowi == 1, h1, pltpu.roll(u, 2, axis=0)))
    cw = cw_ref[...]
    conv = cb_ref[...].astype(F32) * (cw[0:1] * u2 + cw[1:2] * u1 + cw[2:3] * u)

    g = g_ref[...].astype(F32)
    g = 1.0 / (1.0 + jnp.exp(-g))
    merged = (g[:, 0:D_MODEL] * jnp.dot(ret_ref[...], wb_ref[0], preferred_element_type=F32)
              + g[:, D_MODEL:2 * D_MODEL] * jnp.dot(diff_ref[...], wb_ref[1], preferred_element_type=F32)
              + g[:, 2 * D_MODEL:] * jnp.dot(conv.astype(BF16), wb_ref[2], preferred_element_type=F32))
    y = x_ref[...] + jnp.dot(merged.astype(BF16), wo_ref[...], preferred_element_type=F32)
    r = rowi + (start - (start // p_rows) * p_rows)
    is_pad = (r < META_PAD) | ((r >= p_rows) & (r < p_rows + META_PAD))
    o_ref[...] = jnp.where(is_pad, 0.0, y)


def _merge(ret, diff, proj, x, conv_w, w_branch, w_out, p_rows):
    n, d = x.shape
    tm = _largest_divisor(n, (512, 256, 128))
    halo_blocks = tm // BF16_SUBLANES

    def col(c):
        return lambda i: (i, c // d)

    def halo(c):
        return lambda i: (jnp.maximum(i * halo_blocks - 1, 0), c // d)

    return pl.pallas_call(
        functools.partial(_merge_kernel, tm=tm, p_rows=p_rows),
        grid=(n // tm,),
        in_specs=[
            pl.BlockSpec((tm, d), lambda i: (i, 0)),
            pl.BlockSpec((tm, d), lambda i: (i, 0)),
            pl.BlockSpec((tm, d), col(COL_CB)),
            pl.BlockSpec((tm, d), col(COL_CC)),
            pl.BlockSpec((tm, d), col(COL_CX)),
            pl.BlockSpec((BF16_SUBLANES, d), halo(COL_CC)),
            pl.BlockSpec((BF16_SUBLANES, d), halo(COL_CX)),
            pl.BlockSpec((tm, N_BRANCH * d), lambda i: (i, COL_GATE // (N_BRANCH * d))),
            pl.BlockSpec((tm, d), lambda i: (i, 0)),
            pl.BlockSpec((CONV_K, d), lambda i: (0, 0)),
            pl.BlockSpec((N_BRANCH, d, d), lambda i: (0, 0, 0), pipeline_mode=pl.Buffered(1)),
            pl.BlockSpec((d, d), lambda i: (0, 0), pipeline_mode=pl.Buffered(1)),
        ],
        out_specs=pl.BlockSpec((tm, d), lambda i: (i, 0)),
        out_shape=jax.ShapeDtypeStruct((n, d), F32),
        compiler_params=_params("parallel"),
        name="merge",
    )(ret, diff, proj, proj, proj, proj, proj, proj, x, conv_w, w_branch, w_out)


def _mlp_kernel(x_ref, nw_ref, wu_ref, wd_ref, o_ref, *, ff_chunk):
    x = x_ref[...]
    xn = _rms(x, nw_ref[...]).astype(BF16)
    acc = x
    for c in range(wu_ref.shape[1] // ff_chunk):
        cols = slice(c * ff_chunk, (c + 1) * ff_chunk)
        u = jnp.maximum(jnp.dot(xn, wu_ref[:, cols], preferred_element_type=F32), 0.0)
        acc = acc + jnp.dot((u * u).astype(BF16), wd_ref[cols, :], preferred_element_type=F32)
    o_ref[...] = acc


def _mlp(x, norm_w, w_up, w_down):
    n, d = x.shape
    d_ff = w_up.shape[1]
    tm = _largest_divisor(n, (512, 256, 128))
    return pl.pallas_call(
        functools.partial(_mlp_kernel, ff_chunk=1024),
        grid=(n // tm,),
        in_specs=[
            pl.BlockSpec((tm, d), lambda i: (i, 0)),
            pl.BlockSpec((1, d), lambda i: (0, 0)),
            pl.BlockSpec((d, d_ff), lambda i: (0, 0)),
            pl.BlockSpec((d_ff, d), lambda i: (0, 0)),
        ],
        out_specs=pl.BlockSpec((tm, d), lambda i: (i, 0)),
        out_shape=jax.ShapeDtypeStruct((n, d), F32),
        compiler_params=_params("parallel"),
        name="mlp",
    )(x, norm_w, w_up, w_down)


def _final_norm_kernel(x_ref, w_ref, o_ref):
    o_ref[...] = _rms(x_ref[...], w_ref[...])


def _final_norm(h, w, batch, seq, p_rows):
    d = h.shape[1]
    tr = _largest_divisor(seq, (1024, 512, 256, 128))
    steps = seq // tr
    out = pl.pallas_call(
        _final_norm_kernel,
        grid=(batch, steps),
        in_specs=[
            pl.BlockSpec((pl.Element(tr), pl.Element(d)),
                         lambda b, i: (pl.multiple_of(b * p_rows + BLOCK + i * tr, BLOCK), 0)),
            pl.BlockSpec((1, d), lambda b, i: (0, 0)),
        ],
        out_specs=pl.BlockSpec((tr, d), lambda b, i: (b * steps + i, 0)),
        out_shape=jax.ShapeDtypeStruct((batch * seq, d), h.dtype),
        compiler_params=_params("parallel", "parallel"),
        name="final_norm",
    )(h, w)
    return out.reshape(batch, seq, d)


def kernel(x, meta_tokens, norm1_w, w_in, conv_w, diff_lambda, diff_subln_w, w_branch, w_out,
           norm2_w, w_up, w_down, final_norm_w):
    batch, seq, d = x.shape
    depth = w_in.shape[0]
    assert d == D_MODEL and w_in.shape[1:] == (D_MODEL, D_IN) and seq % BLOCK == 0
    p_rows = META_PAD + N_META + seq

    meta = jnp.broadcast_to(meta_tokens.astype(x.dtype)[None], (batch, N_META, d))
    h = jnp.concatenate([jnp.zeros((batch, META_PAD, d), x.dtype), meta, x], axis=1).reshape(batch * p_rows, d)

    layer_ids = jnp.arange(depth, dtype=F32)
    lam_inits = (0.8 - 0.6 * jnp.exp(-0.3 * layer_ids)).reshape(depth, 1, 1)
    w_in = w_in.at[:, :, COL_DQ:COL_DK].multiply(DIFF_HD ** -0.5 * LOG2E)
    layers = dict(
        norm1_w=norm1_w.reshape(depth, 1, d), w_in=w_in.astype(BF16), conv_w=conv_w,
        lam_vecs=diff_lambda.astype(F32), lam_init=lam_inits, subln_w=diff_subln_w.reshape(depth, DIFF_VD, 1),
        w_branch=w_branch.astype(BF16), w_out=w_out.astype(BF16), norm2_w=norm2_w.reshape(depth, 1, d),
        w_up=w_up.astype(BF16), w_down=w_down.astype(BF16))

    def layer(h, p):
        proj = _inproj(h, p["norm1_w"], p["w_in"])
        ret = _retention(proj, batch, p_rows)
        diff = _diff_attention(proj, p["lam_vecs"], p["lam_init"], p["subln_w"], batch, p_rows)
        h = _merge(ret, diff, proj, h, p["conv_w"], p["w_branch"], p["w_out"], p_rows)
        h = _mlp(h, p["norm2_w"], p["w_up"], p["w_down"])
        return h, None

    h, _ = lax.scan(layer, h, layers)
    return _final_norm(h, final_norm_w.reshape(1, d), batch, seq, p_rows)
```

```python
import functools
import math

import jax
import jax.numpy as jnp
from jax import lax
from jax.experimental import pallas as pl
from jax.experimental.pallas import tpu as pltpu

N_META = 16
BLOCK = 128
META_PAD = BLOCK - N_META
RET_HEADS = 4
RET_DK = 128
RET_DV = 256
DIFF_HEADS = 8
DIFF_HD = 64
DIFF_VD = 128
CONV_K = 3
N_BRANCH = 3
EPS = 1e-6
NEG_INF = -1e30

D_MODEL = RET_HEADS * RET_DV
COL_RQ, COL_RK, COL_RV, COL_RG = 0, 512, 1024, 2048
COL_DQ, COL_DK, COL_DV = 3072, 4096, 5120
COL_CB, COL_CC, COL_CX, COL_GATE = 6144, 7168, 8192, 9216
D_IN = COL_GATE + N_BRANCH * D_MODEL

V7X_VMEM_LIMIT_BYTES = 56 * 1024 * 1024
BF16_SUBLANES = 16
ATTN_KEY_TILE = 512
ATTN_MAX_TRIP_TILES = 4
ATTN_HEADS_PER_STEP = 4

F32 = jnp.float32
BF16 = jnp.bfloat16


def _largest_divisor(n, candidates):
    for c in candidates:
        if n % c == 0:
            return c
    raise ValueError(f"no tile in {candidates} divides {n}")


def _params(*semantics):
    return pltpu.CompilerParams(dimension_semantics=semantics, vmem_limit_bytes=V7X_VMEM_LIMIT_BYTES)


def _rms(x, w):
    return x * lax.rsqrt(jnp.mean(x * x, axis=-1, keepdims=True) + EPS) * w


def _inproj_kernel(x_ref, nw_ref, w_ref, o_ref, xn_ref):
    @pl.when(pl.program_id(1) == 0)
    def _():
        xn_ref[...] = _rms(x_ref[...], nw_ref[...]).astype(BF16)

    o_ref[...] = jnp.dot(xn_ref[...], w_ref[...], preferred_element_type=F32).astype(o_ref.dtype)


def _inproj(h, norm_w, w_in):
    n, d = h.shape
    tm = _largest_divisor(n, (1024, 512, 256, 128))
    tn = 2048
    return pl.pallas_call(
        _inproj_kernel,
        grid=(n // tm, D_IN // tn),
        in_specs=[
            pl.BlockSpec((tm, d), lambda i, j: (i, 0)),
            pl.BlockSpec((1, d), lambda i, j: (0, 0)),
            pl.BlockSpec((d, tn), lambda i, j: (0, j)),
        ],
        out_specs=pl.BlockSpec((tm, tn), lambda i, j: (i, j)),
        out_shape=jax.ShapeDtypeStruct((n, D_IN), BF16),
        scratch_shapes=[pltpu.VMEM((tm, d), BF16)],
        compiler_params=_params("parallel", "arbitrary"),
        name="inproj",
    )(h, norm_w, w_in)


def _ret_log_gamma(head):
    return math.log1p(-(2.0 ** (-5.0 - head)))


def _retention_kernel(q_ref, k_ref, v_ref, g_ref, o_ref, state_ref, intra_ref, qd_ref, kd_ref, *, chunks):
    @pl.when(pl.program_id(1) == 0)
    def _():
        state_ref[...] = jnp.zeros_like(state_ref)
        scale = RET_DK ** -0.5
        i_sq = lax.broadcasted_iota(jnp.int32, (BLOCK, BLOCK), 0).astype(F32)
        j_sq = lax.broadcasted_iota(jnp.int32, (BLOCK, BLOCK), 1).astype(F32)
        i_dv = lax.broadcasted_iota(jnp.int32, (BLOCK, RET_DV), 0).astype(F32)
        dist = i_sq - j_sq
        for h in range(RET_HEADS):
            lg = _ret_log_gamma(h)
            intra_ref[h] = jnp.where(dist >= 0, jnp.exp(lg * jnp.maximum(dist, 0.0)), 0.0) * scale
            qd_ref[h] = jnp.exp(lg * (i_dv + 1.0))
            kd_ref[h] = jnp.exp(lg * (BLOCK - 1.0 - i_sq)) * scale

    for c in range(chunks):
        rows = pl.ds(c * BLOCK, BLOCK)
        for h in range(RET_HEADS):
            s_decay = math.exp(_ret_log_gamma(h) * BLOCK)
            q = q_ref[rows, h * RET_DK:(h + 1) * RET_DK]
            k = k_ref[rows, h * RET_DK:(h + 1) * RET_DK]
            v = v_ref[rows, h * RET_DV:(h + 1) * RET_DV]
            st = state_ref[h]
            scores = lax.dot_general(q, k, (((1,), (1,)), ((), ())), preferred_element_type=F32)
            scores = (scores * intra_ref[h]).astype(BF16)
            out = (jnp.dot(scores, v, preferred_element_type=F32)
                   + qd_ref[h] * jnp.dot(q, st.astype(BF16), preferred_element_type=F32))
            kdec = (k.astype(F32) * kd_ref[h]).astype(BF16)
            state_ref[h] = st * s_decay + lax.dot_general(
                kdec, v, (((0,), (0,)), ((), ())), preferred_element_type=F32)
            mu = jnp.mean(out, axis=-1, keepdims=True)
            cen = out - mu
            var = jnp.mean(cen * cen, axis=-1, keepdims=True)
            g = g_ref[rows, h * RET_DV:(h + 1) * RET_DV].astype(F32)
            silu = g / (1.0 + jnp.exp(-g))
            o_ref[rows, h * RET_DV:(h + 1) * RET_DV] = (cen * lax.rsqrt(var + EPS) * silu).astype(o_ref.dtype)


def _retention(proj, batch, p_rows):
    n = proj.shape[0]
    nc = p_rows // BLOCK
    chunks = _largest_divisor(nc, (13, 5, 3, 1))
    steps = nc // chunks
    tr = chunks * BLOCK
    qk_w = RET_HEADS * RET_DK

    def rows(b, c):
        return b * steps + c

    return pl.pallas_call(
        functools.partial(_retention_kernel, chunks=chunks),
        grid=(batch, steps),
        in_specs=[
            pl.BlockSpec((tr, qk_w), lambda b, c: (rows(b, c), COL_RQ // qk_w)),
            pl.BlockSpec((tr, qk_w), lambda b, c: (rows(b, c), COL_RK // qk_w)),
            pl.BlockSpec((tr, D_MODEL), lambda b, c: (rows(b, c), COL_RV // D_MODEL)),
            pl.BlockSpec((tr, D_MODEL), lambda b, c: (rows(b, c), COL_RG // D_MODEL)),
        ],
        out_specs=pl.BlockSpec((tr, D_MODEL), lambda b, c: (rows(b, c), 0)),
        out_shape=jax.ShapeDtypeStruct((n, D_MODEL), BF16),
        scratch_shapes=[
            pltpu.VMEM((RET_HEADS, RET_DK, RET_DV), F32),
            pltpu.VMEM((RET_HEADS, BLOCK, BLOCK), F32),
            pltpu.VMEM((RET_HEADS, BLOCK, RET_DV), F32),
            pltpu.VMEM((RET_HEADS, BLOCK, BLOCK), F32),
        ],
        compiler_params=_params("parallel", "arbitrary"),
        name="retention",
    )(proj, proj, proj, proj)


FEAT_BLK, FEAT_ROW, FEAT_MASK, FEAT_Q0, FEAT_QROW = 0, 3, 6, 7, 10
FEAT_SPLIT = 3
LOG2E = math.log2(math.e)
VT_ROWS = DIFF_VD + BF16_SUBLANES
SKIP_LOG2_MARGIN = 160.0
SKIP_NORM_SLACK = 1.01


def _max_sq_norm(x, map_selector):
    sq = x.astype(F32)
    sq = (sq * sq).astype(BF16)
    per_map = jnp.dot(sq, map_selector, preferred_element_type=F32)
    return jnp.max(jnp.max(per_map, axis=0, keepdims=True), axis=-1, keepdims=True)


def _first_live_tile(q2, k2, head, *, tk, n_tiles):
    blocks_per_tile = tk // BLOCK
    qi = lax.broadcasted_iota(jnp.int32, (BLOCK, BLOCK), 0)
    tj = lax.broadcasted_iota(jnp.int32, (BLOCK, BLOCK), 1)
    diag_tile = lax.shift_right_logical(qi, blocks_per_tile.bit_length() - 1)
    slope2 = jnp.exp2(-(jnp.zeros((BLOCK, BLOCK), F32) + head + 1.0)) * LOG2E
    kd2 = jnp.max(jnp.where(tj == diag_tile, k2, 0.0), axis=-1, keepdims=True)
    dist_min = (qi * BLOCK - tj * tk - (tk - 1)).astype(F32)
    bound = SKIP_NORM_SLACK * jnp.sqrt(q2) * (jnp.sqrt(k2) + jnp.sqrt(kd2)) + SKIP_LOG2_MARGIN
    dead = jnp.where((slope2 * dist_min > bound) & (tj < diag_tile), 1.0, 0.0)
    run = jnp.ones((BLOCK, 1), F32)
    count = jnp.zeros((BLOCK, 1), F32)
    for j in range(n_tiles):
        run = run * dead[:, j:j + 1]
        count = count + run
    return count


def _kvprep_kernel(q_ref, k_ref, v_ref, k_out_ref, vt_ref, js_ref, *, tk, n_real, n_tiles):
    blocks_per_tile = tk // BLOCK
    hw = 2 * DIFF_HD
    lane = lax.broadcasted_iota(jnp.int32, (BLOCK, hw), 1)
    row = lax.broadcasted_iota(jnp.int32, (BLOCK, hw), 0)
    map_selector = jnp.where(((lane == 0) & (row < DIFF_HD)) | ((lane == 1) & (row >= DIFF_HD)), 1.0, 0.0)
    map_selector = map_selector.astype(BF16)
    ones_row = jnp.where(lax.broadcasted_iota(jnp.int32, (VT_ROWS - DIFF_VD, BLOCK), 0) == 0, 1.0, 0.0)
    q2 = jnp.zeros((BLOCK, BLOCK), F32)
    k2 = jnp.zeros((BLOCK, BLOCK), F32)
    for t in range(n_tiles):
        for bi in range(blocks_per_tile):
            blk = t * blocks_per_tile + bi
            rows = pl.ds(bi * BLOCK, BLOCK)
            cols = pl.ds(bi * BLOCK, BLOCK)
            vt_ref[0, t, DIFF_VD:, cols] = ones_row.astype(BF16)
            if blk < n_real:
                src = slice(blk * BLOCK, (blk + 1) * BLOCK)
                kb = k_ref[src, :]
                k_out_ref[0, t, rows, :] = kb
                vt_ref[0, t, 0:DIFF_VD, cols] = v_ref[src, :].astype(F32).T.astype(BF16)
                q2 = jnp.where(row == blk, _max_sq_norm(q_ref[src, :], map_selector), q2)
                k2 = jnp.where(lane == t, jnp.maximum(k2, _max_sq_norm(kb, map_selector)), k2)
            else:
                k_out_ref[0, t, rows, :] = jnp.zeros((BLOCK, hw), BF16)
                vt_ref[0, t, 0:DIFF_VD, cols] = jnp.zeros((DIFF_VD, BLOCK), BF16)
    first = _first_live_tile(q2, k2, pl.program_id(1).astype(F32), tk=tk, n_tiles=n_tiles)
    js_ref[0] = jnp.broadcast_to(first, (BLOCK, BLOCK)).astype(jnp.int32)


def _kvprep(proj, batch, p_rows, tk):
    n_real = p_rows // BLOCK
    n_tiles = -(-p_rows // tk)
    hw = 2 * DIFF_HD
    blocks_per_tile = tk // BLOCK
    assert hw == BLOCK and n_real <= BLOCK and blocks_per_tile & (blocks_per_tile - 1) == 0
    return pl.pallas_call(
        functools.partial(_kvprep_kernel, tk=tk, n_real=n_real, n_tiles=n_tiles),
        grid=(batch, DIFF_HEADS),
        in_specs=[
            pl.BlockSpec((p_rows, hw), lambda b, h: (b, COL_DQ // hw + h)),
            pl.BlockSpec((p_rows, hw), lambda b, h: (b, COL_DK // hw + h)),
            pl.BlockSpec((p_rows, DIFF_VD), lambda b, h: (b, COL_DV // DIFF_VD + h)),
        ],
        out_specs=[
            pl.BlockSpec((1, n_tiles, tk, hw), lambda b, h: (b * DIFF_HEADS + h, 0, 0, 0)),
            pl.BlockSpec((1, n_tiles, VT_ROWS, tk), lambda b, h: (b * DIFF_HEADS + h, 0, 0, 0)),
            pl.BlockSpec((1, BLOCK, BLOCK), lambda b, h: (b * DIFF_HEADS + h, 0, 0)),
        ],
        out_shape=[
            jax.ShapeDtypeStruct((batch * DIFF_HEADS, n_tiles, tk, hw), BF16),
            jax.ShapeDtypeStruct((batch * DIFF_HEADS, n_tiles, VT_ROWS, tk), BF16),
            jax.ShapeDtypeStruct((batch * DIFF_HEADS, BLOCK, BLOCK), jnp.int32),
        ],
        compiler_params=_params("parallel", "parallel"),
        name="kvprep",
    )(proj, proj, proj)


def _attn_kernel(first_ref, q_ref, qn_ref, k_ref, vt_ref, feat_ref, lamv_ref, lami_ref, sw_ref, o_ref,
                 s_ref, smax_ref, m_ref, acc_ref, qcat_ref, *, tk, heads, q_blocks):
    tq = BLOCK
    hp = pl.program_id(1)
    lanes = 2 * tq
    hw = 2 * DIFF_HD
    blocks_per_tile = tk // BLOCK
    n_q_total = pl.num_programs(2) * q_blocks
    head0 = pl.program_id(0) * DIFF_HEADS + hp * heads

    lane_h = lax.broadcasted_iota(jnp.int32, (tq, hw), 1)
    row_q = lax.broadcasted_iota(jnp.int32, (tq, hw), 0).astype(F32)
    lv = lamv_ref[...]
    lam = (jnp.exp(jnp.sum(lv[0:1] * lv[1:2], axis=-1, keepdims=True))
           - jnp.exp(jnp.sum(lv[2:3] * lv[3:4], axis=-1, keepdims=True)) + lami_ref[...])

    def split_into(feat, first_lane, x):
        for piece in range(FEAT_SPLIT):
            head_part = x.astype(BF16).astype(F32)
            feat = jnp.where(lane_h == first_lane + piece, head_part, feat)
            x = x - head_part
        return feat

    def tile_range(qi):
        n_full = lax.shift_right_logical(qi, blocks_per_tile.bit_length() - 1)
        j_first = n_full
        for hh in range(heads):
            j_first = jnp.minimum(j_first, first_ref[(head0 + hh) * n_q_total + qi])
        return n_full, j_first

    def slope_log2(hh, shape):
        head = (hp * heads + hh).astype(F32)
        return jnp.exp2(-(jnp.zeros(shape, F32) + head + 1.0)) * LOG2E

    def block_independent_coefficients(hh):
        slope2 = slope_log2(hh, (tq, hw))
        feat = jnp.where(lane_h == FEAT_MASK, NEG_INF, 0.0)
        feat = split_into(feat, FEAT_BLK, slope2 * BLOCK)
        feat = split_into(feat, FEAT_ROW, slope2)
        feat = split_into(feat, FEAT_QROW, -slope2 * row_q)
        return jnp.concatenate([feat.T, feat.T], axis=1)

    coefficient_row = lax.broadcasted_iota(jnp.int32, (hw, lanes), 0)
    static_coefficients = [block_independent_coefficients(hh) for hh in range(heads)]

    def qcat_of(hh, qi, q_block):
        bottom = static_coefficients[hh]
        x = -slope_log2(hh, (1, lanes)) * (qi * tq).astype(F32)
        for piece in range(FEAT_SPLIT):
            head_part = x.astype(BF16).astype(F32)
            bottom = jnp.where(coefficient_row == FEAT_Q0 + piece, head_part, bottom)
            x = x - head_part
        q = q_block[:, hh * hw:(hh + 1) * hw].astype(F32)
        q_m0 = jnp.where(lane_h < DIFF_HD, q, 0.0)
        q_m1 = jnp.where(lane_h >= DIFF_HD, q, 0.0)
        top = jnp.concatenate([q_m0.T, q_m1.T], axis=1)
        return jnp.concatenate([top, bottom], axis=0).astype(BF16)

    def score_tile(j, slot, qcats):
        for hh in range(heads):
            k_aug = jnp.concatenate([k_ref[hh, j], feat_ref[j]], axis=1)
            s = jnp.dot(k_aug, qcats[hh], preferred_element_type=F32)
            s_ref[slot, hh] = s
            smax_ref[slot, hh] = jnp.max(s, axis=0, keepdims=True)

    def start_block(qi, q_block):
        _, j_first = tile_range(qi)
        qcats = [qcat_of(hh, qi, q_block) for hh in range(heads)]
        for hh in range(heads):
            qcat_ref[hh] = qcats[hh]
        score_tile(j_first, 0, qcats)

    @pl.when(pl.program_id(2) == 0)
    def _():
        start_block(pl.program_id(2) * q_blocks, q_ref[0:tq, :])

    def query_block(sub, carry):
        qi = pl.program_id(2) * q_blocks + sub
        q_rows = pl.ds(pl.multiple_of(sub * tq, tq), tq)
        n_full, j_first = tile_range(qi)
        qcats = [qcat_ref[hh] for hh in range(heads)]

        def scores(j, slot):
            score_tile(j, slot, qcats)

        def accumulate(hh, s, s_max, vt):
            m_old = m_ref[hh]
            m_new = jnp.maximum(m_old, s_max)
            alpha = jnp.exp2(m_old - m_new)
            p = jnp.exp2(s - m_new).astype(BF16)
            m_ref[hh] = m_new
            pv = jnp.dot(vt, p, preferred_element_type=F32)
            acc_ref[hh] = acc_ref[hh] * alpha + pv

        def update(j, slot):
            for hh in range(heads):
                accumulate(hh, s_ref[slot, hh], smax_ref[slot, hh], vt_ref[hh, j])

        def update_diagonal(j, slot, diag_block):
            live = (diag_block + 1) * BLOCK
            k_in_block = lax.broadcasted_iota(jnp.int32, (BLOCK, lanes), 0)
            q_in_block = lax.broadcasted_iota(jnp.int32, (BLOCK, lanes), 1)
            q_in_block = jnp.where(q_in_block >= tq, q_in_block - tq, q_in_block)
            for hh in range(heads):
                s = s_ref[slot, hh, 0:live, :]
                s_diag = jnp.where(k_in_block <= q_in_block, s[live - BLOCK:live], NEG_INF)
                if diag_block > 0:
                    s = jnp.concatenate([s[0:live - BLOCK], s_diag], axis=0)
                else:
                    s = s_diag
                accumulate(hh, s, jnp.max(s, axis=0, keepdims=True), vt_ref[hh, j, :, 0:live])
                o = acc_ref[hh, 0:DIFF_VD] / acc_ref[hh, DIFF_VD:DIFF_VD + 1]
                dlt = o[:, 0:tq] - lam * o[:, tq:lanes]
                ms = jnp.mean(dlt * dlt, axis=0, keepdims=True)
                y = dlt * lax.rsqrt(ms + EPS) * sw_ref[...] * (1.0 - lami_ref[...])
                o_ref[q_rows, hh * DIFF_VD:(hh + 1) * DIFF_VD] = y.T.astype(o_ref.dtype)
            next_rows = pl.ds(pl.multiple_of(jnp.minimum(sub + 1, q_blocks - 1) * tq, tq), tq)
            q_next = jnp.where(sub == q_blocks - 1, qn_ref[...], q_ref[next_rows, :])
            start_block(jnp.minimum(qi + 1, n_q_total - 1), q_next)

        m_ref[...] = jnp.full_like(m_ref, NEG_INF)
        acc_ref[...] = jnp.zeros_like(acc_ref)


        def tiles(j, count):
            for t in range(count):
                scores(j + t + 1, (t + 1) % 2)
                update(j + t, t % 2)

        n_loop = n_full - j_first
        j = j_first
        count = ATTN_MAX_TRIP_TILES
        while count >= 1:
            def body(i, c, base=j, count=count):
                tiles(base + count * i, count)
                return c
            if count == ATTN_MAX_TRIP_TILES:
                trips = lax.shift_right_logical(n_loop, count.bit_length() - 1)
            else:
                trips = lax.shift_right_logical(n_loop & count, count.bit_length() - 1)
            lax.fori_loop(0, trips, body, 0)
            j = j + count * trips
            count //= 2
        diag_block = qi & (blocks_per_tile - 1)
        diag_slot = n_loop & 1
        for c in range(blocks_per_tile):
            pl.when(diag_block == c)(functools.partial(update_diagonal, n_full, diag_slot, c))
        return carry

    lax.fori_loop(0, q_blocks, query_block, 0)


def _position_features(n_tiles, tk, p_rows):
    kpos = jnp.arange(n_tiles * tk, dtype=jnp.int32)[:, None]
    lane = jnp.arange(2 * DIFF_HD, dtype=jnp.int32)[None, :]
    masked = (kpos < META_PAD) | (kpos >= p_rows)
    feat = jnp.where(lane < FEAT_ROW, kpos // BLOCK,
           jnp.where(lane < FEAT_MASK, kpos % BLOCK,
           jnp.where(lane == FEAT_MASK, masked.astype(jnp.int32),
           jnp.where(lane < FEAT_QROW + FEAT_SPLIT, 1, 0))))
    return feat.astype(BF16).reshape(n_tiles, tk, 2 * DIFF_HD)


def _diff_attention(proj, lam_vecs, lam_init, subln_w, batch, p_rows):
    n = proj.shape[0]
    tq = BLOCK
    tk = ATTN_KEY_TILE
    heads = ATTN_HEADS_PER_STEP
    nq = p_rows // tq
    q_blocks = _largest_divisor(nq, (5, 3, 1))
    steps = nq // q_blocks
    tr = q_blocks * tq
    hw = 2 * DIFF_HD
    k_tiles, v_t, first_tile = _kvprep(proj, batch, p_rows, tk)
    first_tile = first_tile[:, :nq, 0].reshape(batch * DIFF_HEADS * nq)
    n_tiles = k_tiles.shape[1]
    feat = _position_features(n_tiles, tk, p_rows)
    groups = DIFF_HEADS // heads
    grid_spec = pltpu.PrefetchScalarGridSpec(
        num_scalar_prefetch=1,
        grid=(batch, groups, steps),
        in_specs=[
            pl.BlockSpec((tr, heads * hw), lambda b, g, i, first: (b * steps + i, COL_DQ // (heads * hw) + g)),
            pl.BlockSpec((tq, heads * hw), lambda b, g, i, first: (
                b * nq + jnp.minimum((i + 1) * q_blocks, nq - 1), COL_DQ // (heads * hw) + g)),
            pl.BlockSpec((heads, n_tiles, tk, hw), lambda b, g, i, first: (b * groups + g, 0, 0, 0)),
            pl.BlockSpec((heads, n_tiles, VT_ROWS, tk), lambda b, g, i, first: (b * groups + g, 0, 0, 0)),
            pl.BlockSpec((n_tiles, tk, hw), lambda b, g, i, first: (0, 0, 0), pipeline_mode=pl.Buffered(1)),
            pl.BlockSpec((4, DIFF_HD), lambda b, g, i, first: (0, 0)),
            pl.BlockSpec((1, 1), lambda b, g, i, first: (0, 0)),
            pl.BlockSpec((DIFF_VD, 1), lambda b, g, i, first: (0, 0)),
        ],
        out_specs=pl.BlockSpec((tr, heads * DIFF_VD), lambda b, g, i, first: (b * steps + i, g)),
        scratch_shapes=[
            pltpu.VMEM((2, heads, tk, 2 * tq), F32),
            pltpu.VMEM((2, heads, 1, 2 * tq), F32),
            pltpu.VMEM((heads, 1, 2 * tq), F32),
            pltpu.VMEM((heads, VT_ROWS, 2 * tq), F32),
            pltpu.VMEM((heads, 2 * hw, 2 * tq), BF16),
        ],
    )
    return pl.pallas_call(
        functools.partial(_attn_kernel, tk=tk, heads=heads, q_blocks=q_blocks),
        grid_spec=grid_spec,
        out_shape=jax.ShapeDtypeStruct((n, D_MODEL), BF16),
        compiler_params=_params("arbitrary", "arbitrary", "arbitrary"),
        name="diffattn",
    )(first_tile, proj, proj, k_tiles, v_t, feat, lam_vecs, lam_init, subln_w)


def _merge_kernel(ret_ref, diff_ref, cb_ref, cc_ref, cx_ref, cch_ref, cxh_ref, g_ref, x_ref,
                  cw_ref, wb_ref, wo_ref, o_ref, *, tm, p_rows):
    start = pl.program_id(0) * tm
    rowi = lax.broadcasted_iota(jnp.int32, (tm, D_MODEL), 0)

    u = cc_ref[...].astype(F32) * cx_ref[...].astype(F32)
    uh = cch_ref[...].astype(F32) * cxh_ref[...].astype(F32)
    h1 = uh[BF16_SUBLANES - 1:BF16_SUBLANES, :]
    h2 = uh[BF16_SUBLANES - 2:BF16_SUBLANES - 1, :]
    u1 = jnp.where(rowi == 0, h1, pltpu.roll(u, 1, axis=0))
    u2 = jnp.where(rowi == 0, h2, jnp.where(rowi == 1, h1, pltpu.roll(u, 2, axis=0)))
    cw = cw_ref[...]
    conv = cb_ref[...].astype(F32) * (cw[0:1] * u2 + cw[1:2] * u1 + cw[2:3] * u)

    g = g_ref[...].astype(F32)
    g = 1.0 / (1.0 + jnp.exp(-g))
    merged = (g[:, 0:D_MODEL] * jnp.dot(ret_ref[...], wb_ref[0], preferred_element_type=F32)
              + g[:, D_MODEL:2 * D_MODEL] * jnp.dot(diff_ref[...], wb_ref[1], preferred_element_type=F32)
              + g[:, 2 * D_MODEL:] * jnp.dot(conv.astype(BF16), wb_ref[2], preferred_element_type=F32))
    y = x_ref[...] + jnp.dot(merged.astype(BF16), wo_ref[...], preferred_element_type=F32)
    r = rowi + (start - (start // p_rows) * p_rows)
    is_pad = (r < META_PAD) | ((r >= p_rows) & (r < p_rows + META_PAD))
    o_ref[...] = jnp.where(is_pad, 0.0, y)


def _merge(ret, diff, proj, x, conv_w, w_branch, w_out, p_rows):
    n, d = x.shape
    tm = _largest_divisor(n, (512, 256, 128))
    halo_blocks = tm // BF16_SUBLANES

    def col(c):
        return lambda i: (i, c // d)

    def halo(c):
        return lambda i: (jnp.maximum(i * halo_blocks - 1, 0), c // d)

    return pl.pallas_call(
        functools.partial(_merge_kernel, tm=tm, p_rows=p_rows),
        grid=(n // tm,),
        in_specs=[
            pl.BlockSpec((tm, d), lambda i: (i, 0)),
            pl.BlockSpec((tm, d), lambda i: (i, 0)),
            pl.BlockSpec((tm, d), col(COL_CB)),
            pl.BlockSpec((tm, d), col(COL_CC)),
            pl.BlockSpec((tm, d), col(COL_CX)),
            pl.BlockSpec((BF16_SUBLANES, d), halo(COL_CC)),
            pl.BlockSpec((BF16_SUBLANES, d), halo(COL_CX)),
            pl.BlockSpec((tm, N_BRANCH * d), lambda i: (i, COL_GATE // (N_BRANCH * d))),
            pl.BlockSpec((tm, d), lambda i: (i, 0)),
            pl.BlockSpec((CONV_K, d), lambda i: (0, 0)),
            pl.BlockSpec((N_BRANCH, d, d), lambda i: (0, 0, 0), pipeline_mode=pl.Buffered(1)),
            pl.BlockSpec((d, d), lambda i: (0, 0), pipeline_mode=pl.Buffered(1)),
        ],
        out_specs=pl.BlockSpec((tm, d), lambda i: (i, 0)),
        out_shape=jax.ShapeDtypeStruct((n, d), F32),
        compiler_params=_params("parallel"),
        name="merge",
    )(ret, diff, proj, proj, proj, proj, proj, proj, x, conv_w, w_branch, w_out)


def _mlp_kernel(x_ref, nw_ref, wu_ref, wd_ref, o_ref, *, ff_chunk):
    x = x_ref[...]
    xn = _rms(x, nw_ref[...]).astype(BF16)
    acc = x
    for c in range(wu_ref.shape[1] // ff_chunk):
        cols = slice(c * ff_chunk, (c + 1) * ff_chunk)
        u = jnp.maximum(jnp.dot(xn, wu_ref[:, cols], preferred_element_type=F32), 0.0)
        acc = acc + jnp.dot((u * u).astype(BF16), wd_ref[cols, :], preferred_element_type=F32)
    o_ref[...] = acc


def _mlp(x, norm_w, w_up, w_down):
    n, d = x.shape
    d_ff = w_up.shape[1]
    tm = _largest_divisor(n, (512, 256, 128))
    return pl.pallas_call(
        functools.partial(_mlp_kernel, ff_chunk=1024),
        grid=(n // tm,),
        in_specs=[
            pl.BlockSpec((tm, d), lambda i: (i, 0)),
            pl.BlockSpec((1, d), lambda i: (0, 0)),
            pl.BlockSpec((d, d_ff), lambda i: (0, 0)),
            pl.BlockSpec((d_ff, d), lambda i: (0, 0)),
        ],
        out_specs=pl.BlockSpec((tm, d), lambda i: (i, 0)),
        out_shape=jax.ShapeDtypeStruct((n, d), F32),
        compiler_params=_params("parallel"),
        name="mlp",
    )(x, norm_w, w_up, w_down)


def _final_norm_kernel(x_ref, w_ref, o_ref):
    o_ref[...] = _rms(x_ref[...], w_ref[...])


def _final_norm(h, w, batch, seq, p_rows):
    d = h.shape[1]
    tr = _largest_divisor(seq, (1024, 512, 256, 128))
    steps = seq // tr
    out = pl.pallas_call(
        _final_norm_kernel,
        grid=(batch, steps),
        in_specs=[
            pl.BlockSpec((pl.Element(tr), pl.Element(d)),
                         lambda b, i: (pl.multiple_of(b * p_rows + BLOCK + i * tr, BLOCK), 0)),
            pl.BlockSpec((1, d), lambda b, i: (0, 0)),
        ],
        out_specs=pl.BlockSpec((tr, d), lambda b, i: (b * steps + i, 0)),
        out_shape=jax.ShapeDtypeStruct((batch * seq, d), h.dtype),
        compiler_params=_params("parallel", "parallel"),
        name="final_norm",
    )(h, w)
    return out.reshape(batch, seq, d)


def kernel(x, meta_tokens, norm1_w, w_in, conv_w, diff_lambda, diff_subln_w, w_branch, w_out,
           norm2_w, w_up, w_down, final_norm_w):
    batch, seq, d = x.shape
    depth = w_in.shape[0]
    assert d == D_MODEL and w_in.shape[1:] == (D_MODEL, D_IN) and seq % BLOCK == 0
    p_rows = META_PAD + N_META + seq

    meta = jnp.broadcast_to(meta_tokens.astype(x.dtype)[None], (batch, N_META, d))
    h = jnp.concatenate([jnp.zeros((batch, META_PAD, d), x.dtype), meta, x], axis=1).reshape(batch * p_rows, d)

    layer_ids = jnp.arange(depth, dtype=F32)
    lam_inits = (0.8 - 0.6 * jnp.exp(-0.3 * layer_ids)).reshape(depth, 1, 1)
    w_in = w_in.at[:, :, COL_DQ:COL_DK].multiply(DIFF_HD ** -0.5 * LOG2E)
    layers = dict(
        norm1_w=norm1_w.reshape(depth, 1, d), w_in=w_in.astype(BF16), conv_w=conv_w,
        lam_vecs=diff_lambda.astype(F32), lam_init=lam_inits, subln_w=diff_subln_w.reshape(depth, DIFF_VD, 1),
        w_branch=w_branch.astype(BF16), w_out=w_out.astype(BF16), norm2_w=norm2_w.reshape(depth, 1, d),
        w_up=w_up.astype(BF16), w_down=w_down.astype(BF16))

    def layer(h, p):
        proj = _inproj(h, p["norm1_w"], p["w_in"])
        ret = _retention(proj, batch, p_rows)
        diff = _diff_attention(proj, p["lam_vecs"], p["lam_init"], p["subln_w"], batch, p_rows)
        h = _merge(ret, diff, proj, h, p["conv_w"], p["w_branch"], p["w_out"], p_rows)
        h = _mlp(h, p["norm2_w"], p["w_up"], p["w_down"])
        return h, None

    h, _ = lax.scan(layer, h, layers)
    return _final_norm(h, final_norm_w.reshape(1, d), batch, seq, p_rows)
```

```python
import functools
import math

import jax
import jax.numpy as jnp
from jax import lax
from jax.experimental import pallas as pl
from jax.experimental.pallas import tpu as pltpu

N_META = 16
BLOCK = 128
META_PAD = BLOCK - N_META
RET_HEADS = 4
RET_DK = 128
RET_DV = 256
DIFF_HEADS = 8
DIFF_HD = 64
DIFF_VD = 128
CONV_K = 3
N_BRANCH = 3
EPS = 1e-6
NEG_INF = -1e30

D_MODEL = RET_HEADS * RET_DV
COL_RQ, COL_RK, COL_RV, COL_RG = 0, 512, 1024, 2048
COL_DQ, COL_DK, COL_DV = 3072, 4096, 5120
COL_CB, COL_CC, COL_CX, COL_GATE = 6144, 7168, 8192, 9216
D_IN = COL_GATE + N_BRANCH * D_MODEL

V7X_VMEM_LIMIT_BYTES = 56 * 1024 * 1024
BF16_SUBLANES = 16
ATTN_KEY_TILE = 512
ATTN_MAX_TRIP_TILES = 4
ATTN_HEADS_PER_STEP = 4

F32 = jnp.float32
BF16 = jnp.bfloat16


def _largest_divisor(n, candidates):
    for c in candidates:
        if n % c == 0:
            return c
    raise ValueError(f"no tile in {candidates} divides {n}")


def _params(*semantics):
    return pltpu.CompilerParams(dimension_semantics=semantics, vmem_limit_bytes=V7X_VMEM_LIMIT_BYTES)


def _rms(x, w):
    return x * lax.rsqrt(jnp.mean(x * x, axis=-1, keepdims=True) + EPS) * w


def _inproj_kernel(x_ref, nw_ref, w_ref, o_ref, xn_ref):
    @pl.when(pl.program_id(1) == 0)
    def _():
        xn_ref[...] = _rms(x_ref[...], nw_ref[...]).astype(BF16)

    o_ref[...] = jnp.dot(xn_ref[...], w_ref[...], preferred_element_type=F32).astype(o_ref.dtype)


def _inproj(h, norm_w, w_in):
    n, d = h.shape
    tm = _largest_divisor(n, (1024, 512, 256, 128))
    tn = 3072
    return pl.pallas_call(
        _inproj_kernel,
        grid=(n // tm, D_IN // tn),
        in_specs=[
            pl.BlockSpec((tm, d), lambda i, j: (i, 0)),
            pl.BlockSpec((1, d), lambda i, j: (0, 0)),
            pl.BlockSpec((d, tn), lambda i, j: (0, j)),
        ],
        out_specs=pl.BlockSpec((tm, tn), lambda i, j: (i, j)),
        out_shape=jax.ShapeDtypeStruct((n, D_IN), BF16),
        scratch_shapes=[pltpu.VMEM((tm, d), BF16)],
        compiler_params=_params("parallel", "arbitrary"),
        name="inproj",
    )(h, norm_w, w_in)


def _ret_log_gamma(head):
    return math.log1p(-(2.0 ** (-5.0 - head)))


def _retention_kernel(q_ref, k_ref, v_ref, g_ref, o_ref, state_ref, intra_ref, qd_ref, kd_ref, *, chunks):
    @pl.when(pl.program_id(1) == 0)
    def _():
        state_ref[...] = jnp.zeros_like(state_ref)
        scale = RET_DK ** -0.5
        i_sq = lax.broadcasted_iota(jnp.int32, (BLOCK, BLOCK), 0).astype(F32)
        j_sq = lax.broadcasted_iota(jnp.int32, (BLOCK, BLOCK), 1).astype(F32)
        i_dv = lax.broadcasted_iota(jnp.int32, (BLOCK, RET_DV), 0).astype(F32)
        dist = i_sq - j_sq
        for h in range(RET_HEADS):
            lg = _ret_log_gamma(h)
            intra_ref[h] = jnp.where(dist >= 0, jnp.exp(lg * jnp.maximum(dist, 0.0)), 0.0) * scale
            qd_ref[h] = jnp.exp(lg * (i_dv + 1.0))
            kd_ref[h] = jnp.exp(lg * (BLOCK - 1.0 - i_sq)) * scale

    for c in range(chunks):
        rows = pl.ds(c * BLOCK, BLOCK)
        for h in range(RET_HEADS):
            s_decay = math.exp(_ret_log_gamma(h) * BLOCK)
            q = q_ref[rows, h * RET_DK:(h + 1) * RET_DK]
            k = k_ref[rows, h * RET_DK:(h + 1) * RET_DK]
            v = v_ref[rows, h * RET_DV:(h + 1) * RET_DV]
            st = state_ref[h]
            scores = lax.dot_general(q, k, (((1,), (1,)), ((), ())), preferred_element_type=F32)
            scores = (scores * intra_ref[h]).astype(BF16)
            out = (jnp.dot(scores, v, preferred_element_type=F32)
                   + qd_ref[h] * jnp.dot(q, st.astype(BF16), preferred_element_type=F32))
            kdec = (k.astype(F32) * kd_ref[h]).astype(BF16)
            state_ref[h] = st * s_decay + lax.dot_general(
                kdec, v, (((0,), (0,)), ((), ())), preferred_element_type=F32)
            mu = jnp.mean(out, axis=-1, keepdims=True)
            cen = out - mu
            var = jnp.mean(cen * cen, axis=-1, keepdims=True)
            g = g_ref[rows, h * RET_DV:(h + 1) * RET_DV].astype(F32)
            silu = g / (1.0 + jnp.exp(-g))
            o_ref[rows, h * RET_DV:(h + 1) * RET_DV] = (cen * lax.rsqrt(var + EPS) * silu).astype(o_ref.dtype)


def _retention(proj, batch, p_rows):
    n = proj.shape[0]
    nc = p_rows // BLOCK
    chunks = _largest_divisor(nc, (13, 5, 3, 1))
    steps = nc // chunks
    tr = chunks * BLOCK
    qk_w = RET_HEADS * RET_DK

    def rows(b, c):
        return b * steps + c

    return pl.pallas_call(
        functools.partial(_retention_kernel, chunks=chunks),
        grid=(batch, steps),
        in_specs=[
            pl.BlockSpec((tr, qk_w), lambda b, c: (rows(b, c), COL_RQ // qk_w)),
            pl.BlockSpec((tr, qk_w), lambda b, c: (rows(b, c), COL_RK // qk_w)),
            pl.BlockSpec((tr, D_MODEL), lambda b, c: (rows(b, c), COL_RV // D_MODEL)),
            pl.BlockSpec((tr, D_MODEL), lambda b, c: (rows(b, c), COL_RG // D_MODEL)),
        ],
        out_specs=pl.BlockSpec((tr, D_MODEL), lambda b, c: (rows(b, c), 0)),
        out_shape=jax.ShapeDtypeStruct((n, D_MODEL), BF16),
        scratch_shapes=[
            pltpu.VMEM((RET_HEADS, RET_DK, RET_DV), F32),
            pltpu.VMEM((RET_HEADS, BLOCK, BLOCK), F32),
            pltpu.VMEM((RET_HEADS, BLOCK, RET_DV), F32),
            pltpu.VMEM((RET_HEADS, BLOCK, BLOCK), F32),
        ],
        compiler_params=_params("parallel", "arbitrary"),
        name="retention",
    )(proj, proj, proj, proj)


FEAT_BLK, FEAT_ROW, FEAT_MASK, FEAT_Q0, FEAT_QROW = 0, 3, 6, 7, 10
FEAT_SPLIT = 3
LOG2E = math.log2(math.e)
VT_ROWS = DIFF_VD + BF16_SUBLANES
SKIP_LOG2_MARGIN = 160.0
SKIP_NORM_SLACK = 1.01


def _max_sq_norm(x, map_selector):
    sq = x.astype(F32)
    sq = (sq * sq).astype(BF16)
    per_map = jnp.dot(sq, map_selector, preferred_element_type=F32)
    return jnp.max(jnp.max(per_map, axis=0, keepdims=True), axis=-1, keepdims=True)


def _first_live_tile(q2, k2, head, *, tk, n_tiles):
    blocks_per_tile = tk // BLOCK
    qi = lax.broadcasted_iota(jnp.int32, (BLOCK, BLOCK), 0)
    tj = lax.broadcasted_iota(jnp.int32, (BLOCK, BLOCK), 1)
    diag_tile = lax.shift_right_logical(qi, blocks_per_tile.bit_length() - 1)
    slope2 = jnp.exp2(-(jnp.zeros((BLOCK, BLOCK), F32) + head + 1.0)) * LOG2E
    kd2 = jnp.max(jnp.where(tj == diag_tile, k2, 0.0), axis=-1, keepdims=True)
    dist_min = (qi * BLOCK - tj * tk - (tk - 1)).astype(F32)
    bound = SKIP_NORM_SLACK * jnp.sqrt(q2) * (jnp.sqrt(k2) + jnp.sqrt(kd2)) + SKIP_LOG2_MARGIN
    dead = jnp.where((slope2 * dist_min > bound) & (tj < diag_tile), 1.0, 0.0)
    run = jnp.ones((BLOCK, 1), F32)
    count = jnp.zeros((BLOCK, 1), F32)
    for j in range(n_tiles):
        run = run * dead[:, j:j + 1]
        count = count + run
    return count


def _kvprep_kernel(q_ref, k_ref, v_ref, k_out_ref, vt_ref, js_ref, *, tk, n_real, n_tiles):
    blocks_per_tile = tk // BLOCK
    hw = 2 * DIFF_HD
    lane = lax.broadcasted_iota(jnp.int32, (BLOCK, hw), 1)
    row = lax.broadcasted_iota(jnp.int32, (BLOCK, hw), 0)
    map_selector = jnp.where(((lane == 0) & (row < DIFF_HD)) | ((lane == 1) & (row >= DIFF_HD)), 1.0, 0.0)
    map_selector = map_selector.astype(BF16)
    ones_row = jnp.where(lax.broadcasted_iota(jnp.int32, (VT_ROWS - DIFF_VD, BLOCK), 0) == 0, 1.0, 0.0)
    q2 = jnp.zeros((BLOCK, BLOCK), F32)
    k2 = jnp.zeros((BLOCK, BLOCK), F32)
    for t in range(n_tiles):
        for bi in range(blocks_per_tile):
            blk = t * blocks_per_tile + bi
            rows = pl.ds(bi * BLOCK, BLOCK)
            cols = pl.ds(bi * BLOCK, BLOCK)
            vt_ref[0, t, DIFF_VD:, cols] = ones_row.astype(BF16)
            if blk < n_real:
                src = slice(blk * BLOCK, (blk + 1) * BLOCK)
                kb = k_ref[src, :]
                k_out_ref[0, t, rows, :] = kb
                vt_ref[0, t, 0:DIFF_VD, cols] = v_ref[src, :].astype(F32).T.astype(BF16)
                q2 = jnp.where(row == blk, _max_sq_norm(q_ref[src, :], map_selector), q2)
                k2 = jnp.where(lane == t, jnp.maximum(k2, _max_sq_norm(kb, map_selector)), k2)
            else:
                k_out_ref[0, t, rows, :] = jnp.zeros((BLOCK, hw), BF16)
                vt_ref[0, t, 0:DIFF_VD, cols] = jnp.zeros((DIFF_VD, BLOCK), BF16)
    first = _first_live_tile(q2, k2, pl.program_id(1).astype(F32), tk=tk, n_tiles=n_tiles)
    js_ref[0] = jnp.broadcast_to(first, (BLOCK, BLOCK)).astype(jnp.int32)


def _kvprep(proj, batch, p_rows, tk):
    n_real = p_rows // BLOCK
    n_tiles = -(-p_rows // tk)
    hw = 2 * DIFF_HD
    blocks_per_tile = tk // BLOCK
    assert hw == BLOCK and n_real <= BLOCK and blocks_per_tile & (blocks_per_tile - 1) == 0
    return pl.pallas_call(
        functools.partial(_kvprep_kernel, tk=tk, n_real=n_real, n_tiles=n_tiles),
        grid=(batch, DIFF_HEADS),
        in_specs=[
            pl.BlockSpec((p_rows, hw), lambda b, h: (b, COL_DQ // hw + h)),
            pl.BlockSpec((p_rows, hw), lambda b, h: (b, COL_DK // hw + h)),
            pl.BlockSpec((p_rows, DIFF_VD), lambda b, h: (b, COL_DV // DIFF_VD + h)),
        ],
        out_specs=[
            pl.BlockSpec((1, n_tiles, tk, hw), lambda b, h: (b * DIFF_HEADS + h, 0, 0, 0)),
            pl.BlockSpec((1, n_tiles, VT_ROWS, tk), lambda b, h: (b * DIFF_HEADS + h, 0, 0, 0)),
            pl.BlockSpec((1, BLOCK, BLOCK), lambda b, h: (b * DIFF_HEADS + h, 0, 0)),
        ],
        out_shape=[
            jax.ShapeDtypeStruct((batch * DIFF_HEADS, n_tiles, tk, hw), BF16),
            jax.ShapeDtypeStruct((batch * DIFF_HEADS, n_tiles, VT_ROWS, tk), BF16),
            jax.ShapeDtypeStruct((batch * DIFF_HEADS, BLOCK, BLOCK), jnp.int32),
        ],
        compiler_params=_params("parallel", "parallel"),
        name="kvprep",
    )(proj, proj, proj)


def _attn_kernel(first_ref, q_ref, qn_ref, k_ref, vt_ref, feat_ref, lamv_ref, lami_ref, sw_ref, o_ref,
                 s_ref, smax_ref, m_ref, acc_ref, qcat_ref, *, tk, heads, q_blocks):
    tq = BLOCK
    hp = pl.program_id(1)
    lanes = 2 * tq
    hw = 2 * DIFF_HD
    blocks_per_tile = tk // BLOCK
    n_q_total = pl.num_programs(2) * q_blocks
    head0 = pl.program_id(0) * DIFF_HEADS + hp * heads

    lane_h = lax.broadcasted_iota(jnp.int32, (tq, hw), 1)
    row_q = lax.broadcasted_iota(jnp.int32, (tq, hw), 0).astype(F32)
    lv = lamv_ref[...]
    lam = (jnp.exp(jnp.sum(lv[0:1] * lv[1:2], axis=-1, keepdims=True))
           - jnp.exp(jnp.sum(lv[2:3] * lv[3:4], axis=-1, keepdims=True)) + lami_ref[...])

    def split_into(feat, first_lane, x):
        for piece in range(FEAT_SPLIT):
            head_part = x.astype(BF16).astype(F32)
            feat = jnp.where(lane_h == first_lane + piece, head_part, feat)
            x = x - head_part
        return feat

    def tile_range(qi):
        n_full = lax.shift_right_logical(qi, blocks_per_tile.bit_length() - 1)
        j_first = n_full
        for hh in range(heads):
            j_first = jnp.minimum(j_first, first_ref[(head0 + hh) * n_q_total + qi])
        return n_full, j_first

    def slope_log2(hh, shape):
        head = (hp * heads + hh).astype(F32)
        return jnp.exp2(-(jnp.zeros(shape, F32) + head + 1.0)) * LOG2E

    def block_independent_coefficients(hh):
        slope2 = slope_log2(hh, (tq, hw))
        feat = jnp.where(lane_h == FEAT_MASK, NEG_INF, 0.0)
        feat = split_into(feat, FEAT_BLK, slope2 * BLOCK)
        feat = split_into(feat, FEAT_ROW, slope2)
        feat = split_into(feat, FEAT_QROW, -slope2 * row_q)
        return jnp.concatenate([feat.T, feat.T], axis=1)

    coefficient_row = lax.broadcasted_iota(jnp.int32, (hw, lanes), 0)
    static_coefficients = [block_independent_coefficients(hh) for hh in range(heads)]

    def qcat_of(hh, qi, q_block):
        bottom = static_coefficients[hh]
        x = -slope_log2(hh, (1, lanes)) * (qi * tq).astype(F32)
        for piece in range(FEAT_SPLIT):
            head_part = x.astype(BF16).astype(F32)
            bottom = jnp.where(coefficient_row == FEAT_Q0 + piece, head_part, bottom)
            x = x - head_part
        q = q_block[:, hh * hw:(hh + 1) * hw].astype(F32)
        q_m0 = jnp.where(lane_h < DIFF_HD, q, 0.0)
        q_m1 = jnp.where(lane_h >= DIFF_HD, q, 0.0)
        top = jnp.concatenate([q_m0.T, q_m1.T], axis=1)
        return jnp.concatenate([top, bottom], axis=0).astype(BF16)

    def score_tile(j, slot, qcats):
        for hh in range(heads):
            k_aug = jnp.concatenate([k_ref[hh, j], feat_ref[j]], axis=1)
            s = jnp.dot(k_aug, qcats[hh], preferred_element_type=F32)
            s_ref[slot, hh] = s
            smax_ref[slot, hh] = jnp.max(s, axis=0, keepdims=True)

    def start_block(qi, q_block):
        _, j_first = tile_range(qi)
        qcats = [qcat_of(hh, qi, q_block) for hh in range(heads)]
        for hh in range(heads):
            qcat_ref[hh] = qcats[hh]
        score_tile(j_first, 0, qcats)

    @pl.when(pl.program_id(2) == 0)
    def _():
        start_block(pl.program_id(2) * q_blocks, q_ref[0:tq, :])

    def query_block(sub, carry):
        qi = pl.program_id(2) * q_blocks + sub
        q_rows = pl.ds(pl.multiple_of(sub * tq, tq), tq)
        n_full, j_first = tile_range(qi)
        qcats = [qcat_ref[hh] for hh in range(heads)]

        def scores(j, slot):
            score_tile(j, slot, qcats)

        def accumulate(hh, s, s_max, vt):
            m_old = m_ref[hh]
            m_new = jnp.maximum(m_old, s_max)
            alpha = jnp.exp2(m_old - m_new)
            p = jnp.exp2(s - m_new).astype(BF16)
            m_ref[hh] = m_new
            pv = jnp.dot(vt, p, preferred_element_type=F32)
            acc_ref[hh] = acc_ref[hh] * alpha + pv

        def update(j, slot):
            for hh in range(heads):
                accumulate(hh, s_ref[slot, hh], smax_ref[slot, hh], vt_ref[hh, j])

        def update_diagonal(j, slot, diag_block):
            live = (diag_block + 1) * BLOCK
            k_in_block = lax.broadcasted_iota(jnp.int32, (BLOCK, lanes), 0)
            q_in_block = lax.broadcasted_iota(jnp.int32, (BLOCK, lanes), 1)
            q_in_block = jnp.where(q_in_block >= tq, q_in_block - tq, q_in_block)
            for hh in range(heads):
                s = s_ref[slot, hh, 0:live, :]
                s_diag = jnp.where(k_in_block <= q_in_block, s[live - BLOCK:live], NEG_INF)
                if diag_block > 0:
                    s = jnp.concatenate([s[0:live - BLOCK], s_diag], axis=0)
                else:
                    s = s_diag
                accumulate(hh, s, jnp.max(s, axis=0, keepdims=True), vt_ref[hh, j, :, 0:live])
                o = acc_ref[hh, 0:DIFF_VD] / acc_ref[hh, DIFF_VD:DIFF_VD + 1]
                dlt = o[:, 0:tq] - lam * o[:, tq:lanes]
                ms = jnp.mean(dlt * dlt, axis=0, keepdims=True)
                y = dlt * lax.rsqrt(ms + EPS) * sw_ref[...] * (1.0 - lami_ref[...])
                o_ref[q_rows, hh * DIFF_VD:(hh + 1) * DIFF_VD] = y.T.astype(o_ref.dtype)
            next_rows = pl.ds(pl.multiple_of(jnp.minimum(sub + 1, q_blocks - 1) * tq, tq), tq)
            q_next = jnp.where(sub == q_blocks - 1, qn_ref[...], q_ref[next_rows, :])
            start_block(jnp.minimum(qi + 1, n_q_total - 1), q_next)

        m_ref[...] = jnp.full_like(m_ref, NEG_INF)
        acc_ref[...] = jnp.zeros_like(acc_ref)


        def tiles(j, count):
            for t in range(count):
                scores(j + t + 1, (t + 1) % 2)
                update(j + t, t % 2)

        n_loop = n_full - j_first
        j = j_first
        count = ATTN_MAX_TRIP_TILES
        while count >= 1:
            def body(i, c, base=j, count=count):
                tiles(base + count * i, count)
                return c
            if count == ATTN_MAX_TRIP_TILES:
                trips = lax.shift_right_logical(n_loop, count.bit_length() - 1)
            else:
                trips = lax.shift_right_logical(n_loop & count, count.bit_length() - 1)
            lax.fori_loop(0, trips, body, 0)
            j = j + count * trips
            count //= 2
        diag_block = qi & (blocks_per_tile - 1)
        diag_slot = n_loop & 1
        for c in range(blocks_per_tile):
            pl.when(diag_block == c)(functools.partial(update_diagonal, n_full, diag_slot, c))
        return carry

    lax.fori_loop(0, q_blocks, query_block, 0)


def _position_features(n_tiles, tk, p_rows):
    kpos = jnp.arange(n_tiles * tk, dtype=jnp.int32)[:, None]
    lane = jnp.arange(2 * DIFF_HD, dtype=jnp.int32)[None, :]
    masked = (kpos < META_PAD) | (kpos >= p_rows)
    feat = jnp.where(lane < FEAT_ROW, kpos // BLOCK,
           jnp.where(lane < FEAT_MASK, kpos % BLOCK,
           jnp.where(lane == FEAT_MASK, masked.astype(jnp.int32),
           jnp.where(lane < FEAT_QROW + FEAT_SPLIT, 1, 0))))
    return feat.astype(BF16).reshape(n_tiles, tk, 2 * DIFF_HD)


def _diff_attention(proj, lam_vecs, lam_init, subln_w, batch, p_rows):
    n = proj.shape[0]
    tq = BLOCK
    tk = ATTN_KEY_TILE
    heads = ATTN_HEADS_PER_STEP
    nq = p_rows // tq
    q_blocks = _largest_divisor(nq, (13, 5, 3, 1))
    steps = nq // q_blocks
    tr = q_blocks * tq
    hw = 2 * DIFF_HD
    k_tiles, v_t, first_tile = _kvprep(proj, batch, p_rows, tk)
    first_tile = first_tile[:, :nq, 0].reshape(batch * DIFF_HEADS * nq)
    n_tiles = k_tiles.shape[1]
    feat = _position_features(n_tiles, tk, p_rows)
    groups = DIFF_HEADS // heads
    grid_spec = pltpu.PrefetchScalarGridSpec(
        num_scalar_prefetch=1,
        grid=(batch, groups, steps),
        in_specs=[
            pl.BlockSpec((tr, heads * hw), lambda b, g, i, first: (b * steps + i, COL_DQ // (heads * hw) + g)),
            pl.BlockSpec((tq, heads * hw), lambda b, g, i, first: (
                b * nq + jnp.minimum((i + 1) * q_blocks, nq - 1), COL_DQ // (heads * hw) + g)),
            pl.BlockSpec((heads, n_tiles, tk, hw), lambda b, g, i, first: (b * groups + g, 0, 0, 0)),
            pl.BlockSpec((heads, n_tiles, VT_ROWS, tk), lambda b, g, i, first: (b * groups + g, 0, 0, 0)),
            pl.BlockSpec((n_tiles, tk, hw), lambda b, g, i, first: (0, 0, 0), pipeline_mode=pl.Buffered(1)),
            pl.BlockSpec((4, DIFF_HD), lambda b, g, i, first: (0, 0)),
            pl.BlockSpec((1, 1), lambda b, g, i, first: (0, 0)),
            pl.BlockSpec((DIFF_VD, 1), lambda b, g, i, first: (0, 0)),
        ],
        out_specs=pl.BlockSpec((tr, heads * DIFF_VD), lambda b, g, i, first: (b * steps + i, g)),
        scratch_shapes=[
            pltpu.VMEM((2, heads, tk, 2 * tq), F32),
            pltpu.VMEM((2, heads, 1, 2 * tq), F32),
            pltpu.VMEM((heads, 1, 2 * tq), F32),
            pltpu.VMEM((heads, VT_ROWS, 2 * tq), F32),
            pltpu.VMEM((heads, 2 * hw, 2 * tq), BF16),
        ],
    )
    return pl.pallas_call(
        functools.partial(_attn_kernel, tk=tk, heads=heads, q_blocks=q_blocks),
        grid_spec=grid_spec,
        out_shape=jax.ShapeDtypeStruct((n, D_MODEL), BF16),
        compiler_params=_params("arbitrary", "arbitrary", "arbitrary"),
        name="diffattn",
    )(first_tile, proj, proj, k_tiles, v_t, feat, lam_vecs, lam_init, subln_w)


def _merge_kernel(ret_ref, diff_ref, cb_ref, cc_ref, cx_ref, cch_ref, cxh_ref, g_ref, x_ref,
                  cw_ref, wb_ref, wo_ref, o_ref, *, tm, p_rows):
    start = pl.program_id(0) * tm
    rowi = lax.broadcasted_iota(jnp.int32, (tm, D_MODEL), 0)

    u = cc_ref[...].astype(F32) * cx_ref[...].astype(F32)
    uh = cch_ref[...].astype(F32) * cxh_ref[...].astype(F32)
    h1 = uh[BF16_SUBLANES - 1:BF16_SUBLANES, :]
    h2 = uh[BF16_SUBLANES - 2:BF16_SUBLANES - 1, :]
    u1 = jnp.where(rowi == 0, h1, pltpu.roll(u, 1, axis=0))
    u2 = jnp.where(rowi == 0, h2, jnp.where(rowi == 1, h1, pltpu.roll(u, 2, axis=0)))
    cw = cw_ref[...]
    conv = cb_ref[...].astype(F32) * (cw[0:1] * u2 + cw[1:2] * u1 + cw[2:3] * u)

    g = g_ref[...].astype(F32)
    g = 1.0 / (1.0 + jnp.exp(-g))
    merged = (g[:, 0:D_MODEL] * jnp.dot(ret_ref[...], wb_ref[0], preferred_element_type=F32)
              + g[:, D_MODEL:2 * D_MODEL] * jnp.dot(diff_ref[...], wb_ref[1], preferred_element_type=F32)
              + g[:, 2 * D_MODEL:] * jnp.dot(conv.astype(BF16), wb_ref[2], preferred_element_type=F32))
    y = x_ref[...] + jnp.dot(merged.astype(BF16), wo_ref[...], preferred_element_type=F32)
    r = rowi + (start - (start // p_rows) * p_rows)
    is_pad = (r < META_PAD) | ((r >= p_rows) & (r < p_rows + META_PAD))
    o_ref[...] = jnp.where(is_pad, 0.0, y)


def _merge(ret, diff, proj, x, conv_w, w_branch, w_out, p_rows):
    n, d = x.shape
    tm = _largest_divisor(n, (512, 256, 128))
    halo_blocks = tm // BF16_SUBLANES

    def col(c):
        return lambda i: (i, c // d)

    def halo(c):
        return lambda i: (jnp.maximum(i * halo_blocks - 1, 0), c // d)

    return pl.pallas_call(
        functools.partial(_merge_kernel, tm=tm, p_rows=p_rows),
        grid=(n // tm,),
        in_specs=[
            pl.BlockSpec((tm, d), lambda i: (i, 0)),
            pl.BlockSpec((tm, d), lambda i: (i, 0)),
            pl.BlockSpec((tm, d), col(COL_CB)),
            pl.BlockSpec((tm, d), col(COL_CC)),
            pl.BlockSpec((tm, d), col(COL_CX)),
            pl.BlockSpec((BF16_SUBLANES, d), halo(COL_CC)),
            pl.BlockSpec((BF16_SUBLANES, d), halo(COL_CX)),
            pl.BlockSpec((tm, N_BRANCH * d), lambda i: (i, COL_GATE // (N_BRANCH * d))),
            pl.BlockSpec((tm, d), lambda i: (i, 0)),
            pl.BlockSpec((CONV_K, d), lambda i: (0, 0)),
            pl.BlockSpec((N_BRANCH, d, d), lambda i: (0, 0, 0), pipeline_mode=pl.Buffered(1)),
            pl.BlockSpec((d, d), lambda i: (0, 0), pipeline_mode=pl.Buffered(1)),
        ],
        out_specs=pl.BlockSpec((tm, d), lambda i: (i, 0)),
        out_shape=jax.ShapeDtypeStruct((n, d), F32),
        compiler_params=_params("parallel"),
        name="merge",
    )(ret, diff, proj, proj, proj, proj, proj, proj, x, conv_w, w_branch, w_out)


def _mlp_kernel(x_ref, nw_ref, wu_ref, wd_ref, o_ref, *, ff_chunk):
    x = x_ref[...]
    xn = _rms(x, nw_ref[...]).astype(BF16)
    acc = x
    for c in range(wu_ref.shape[1] // ff_chunk):
        cols = slice(c * ff_chunk, (c + 1) * ff_chunk)
        u = jnp.maximum(jnp.dot(xn, wu_ref[:, cols], preferred_element_type=F32), 0.0)
        acc = acc + jnp.dot((u * u).astype(BF16), wd_ref[cols, :], preferred_element_type=F32)
    o_ref[...] = acc


def _mlp(x, norm_w, w_up, w_down):
    n, d = x.shape
    d_ff = w_up.shape[1]
    tm = _largest_divisor(n, (512, 256, 128))
    return pl.pallas_call(
        functools.partial(_mlp_kernel, ff_chunk=1024),
        grid=(n // tm,),
        in_specs=[
            pl.BlockSpec((tm, d), lambda i: (i, 0)),
            pl.BlockSpec((1, d), lambda i: (0, 0)),
            pl.BlockSpec((d, d_ff), lambda i: (0, 0)),
            pl.BlockSpec((d_ff, d), lambda i: (0, 0)),
        ],
        out_specs=pl.BlockSpec((tm, d), lambda i: (i, 0)),
        out_shape=jax.ShapeDtypeStruct((n, d), F32),
        compiler_params=_params("parallel"),
        name="mlp",
    )(x, norm_w, w_up, w_down)


def _final_norm_kernel(x_ref, w_ref, o_ref):
    o_ref[...] = _rms(x_ref[...], w_ref[...])


def _final_norm(h, w, batch, seq, p_rows):
    d = h.shape[1]
    tr = _largest_divisor(seq, (1024, 512, 256, 128))
    steps = seq // tr
    out = pl.pallas_call(
        _final_norm_kernel,
        grid=(batch, steps),
        in_specs=[
            pl.BlockSpec((pl.Element(tr), pl.Element(d)),
                         lambda b, i: (pl.multiple_of(b * p_rows + BLOCK + i * tr, BLOCK), 0)),
            pl.BlockSpec((1, d), lambda b, i: (0, 0)),
        ],
        out_specs=pl.BlockSpec((tr, d), lambda b, i: (b * steps + i, 0)),
        out_shape=jax.ShapeDtypeStruct((batch * seq, d), h.dtype),
        compiler_params=_params("parallel", "parallel"),
        name="final_norm",
    )(h, w)
    return out.reshape(batch, seq, d)


def kernel(x, meta_tokens, norm1_w, w_in, conv_w, diff_lambda, diff_subln_w, w_branch, w_out,
           norm2_w, w_up, w_down, final_norm_w):
    batch, seq, d = x.shape
    depth = w_in.shape[0]
    assert d == D_MODEL and w_in.shape[1:] == (D_MODEL, D_IN) and seq % BLOCK == 0
    p_rows = META_PAD + N_META + seq

    meta = jnp.broadcast_to(meta_tokens.astype(x.dtype)[None], (batch, N_META, d))
    h = jnp.concatenate([jnp.zeros((batch, META_PAD, d), x.dtype), meta, x], axis=1).reshape(batch * p_rows, d)

    layer_ids = jnp.arange(depth, dtype=F32)
    lam_inits = (0.8 - 0.6 * jnp.exp(-0.3 * layer_ids)).reshape(depth, 1, 1)
    w_in = w_in.at[:, :, COL_DQ:COL_DK].multiply(DIFF_HD ** -0.5 * LOG2E)
    layers = dict(
        norm1_w=norm1_w.reshape(depth, 1, d), w_in=w_in.astype(BF16), conv_w=conv_w,
        lam_vecs=diff_lambda.astype(F32), lam_init=lam_inits, subln_w=diff_subln_w.reshape(depth, DIFF_VD, 1),
        w_branch=w_branch.astype(BF16), w_out=w_out.astype(BF16), norm2_w=norm2_w.reshape(depth, 1, d),
        w_up=w_up.astype(BF16), w_down=w_down.astype(BF16))

    def layer(h, p):
        proj = _inproj(h, p["norm1_w"], p["w_in"])
        ret = _retention(proj, batch, p_rows)
        diff = _diff_attention(proj, p["lam_vecs"], p["lam_init"], p["subln_w"], batch, p_rows)
        h = _merge(ret, diff, proj, h, p["conv_w"], p["w_branch"], p["w_out"], p_rows)
        h = _mlp(h, p["norm2_w"], p["w_up"], p["w_down"])
        return h, None

    h, _ = lax.scan(layer, h, layers)
    return _final_norm(h, final_norm_w.reshape(1, d), batch, seq, p_rows)
```

```python
import functools
import math

import jax
import jax.numpy as jnp
from jax import lax
from jax.experimental import pallas as pl
from jax.experimental.pallas import tpu as pltpu

N_META = 16
BLOCK = 128
META_PAD = BLOCK - N_META
RET_HEADS = 4
RET_DK = 128
RET_DV = 256
DIFF_HEADS = 8
DIFF_HD = 64
DIFF_VD = 128
CONV_K = 3
N_BRANCH = 3
EPS = 1e-6
NEG_INF = -1e30

D_MODEL = RET_HEADS * RET_DV
COL_RQ, COL_RK, COL_RV, COL_RG = 0, 512, 1024, 2048
COL_DQ, COL_DK, COL_DV = 3072, 4096, 5120
COL_CB, COL_CC, COL_CX, COL_GATE = 6144, 7168, 8192, 9216
D_IN = COL_GATE + N_BRANCH * D_MODEL

V7X_VMEM_LIMIT_BYTES = 56 * 1024 * 1024
BF16_SUBLANES = 16
ATTN_KEY_TILE = 512
ATTN_MAX_TRIP_TILES = 4
ATTN_HEADS_PER_STEP = 4

F32 = jnp.float32
BF16 = jnp.bfloat16


def _largest_divisor(n, candidates):
    for c in candidates:
        if n % c == 0:
            return c
    raise ValueError(f"no tile in {candidates} divides {n}")


def _params(*semantics):
    return pltpu.CompilerParams(dimension_semantics=semantics, vmem_limit_bytes=V7X_VMEM_LIMIT_BYTES)


def _rms(x, w):
    return x * lax.rsqrt(jnp.mean(x * x, axis=-1, keepdims=True) + EPS) * w


def _inproj_kernel(x_ref, nw_ref, w_ref, o_ref, xn_ref):
    @pl.when(pl.program_id(1) == 0)
    def _():
        xn_ref[...] = _rms(x_ref[...], nw_ref[...]).astype(BF16)

    o_ref[...] = jnp.dot(xn_ref[...], w_ref[...], preferred_element_type=F32).astype(o_ref.dtype)


def _inproj(h, norm_w, w_in):
    n, d = h.shape
    tm = _largest_divisor(n, (1024, 512, 256, 128))
    tn = 4096
    return pl.pallas_call(
        _inproj_kernel,
        grid=(n // tm, D_IN // tn),
        in_specs=[
            pl.BlockSpec((tm, d), lambda i, j: (i, 0)),
            pl.BlockSpec((1, d), lambda i, j: (0, 0)),
            pl.BlockSpec((d, tn), lambda i, j: (0, j)),
        ],
        out_specs=pl.BlockSpec((tm, tn), lambda i, j: (i, j)),
        out_shape=jax.ShapeDtypeStruct((n, D_IN), BF16),
        scratch_shapes=[pltpu.VMEM((tm, d), BF16)],
        compiler_params=_params("parallel", "arbitrary"),
        name="inproj",
    )(h, norm_w, w_in)


def _ret_log_gamma(head):
    return math.log1p(-(2.0 ** (-5.0 - head)))


def _retention_kernel(q_ref, k_ref, v_ref, g_ref, o_ref, state_ref, intra_ref, qd_ref, kd_ref, *, chunks):
    @pl.when(pl.program_id(1) == 0)
    def _():
        state_ref[...] = jnp.zeros_like(state_ref)
        scale = RET_DK ** -0.5
        i_sq = lax.broadcasted_iota(jnp.int32, (BLOCK, BLOCK), 0).astype(F32)
        j_sq = lax.broadcasted_iota(jnp.int32, (BLOCK, BLOCK), 1).astype(F32)
        i_dv = lax.broadcasted_iota(jnp.int32, (BLOCK, RET_DV), 0).astype(F32)
        dist = i_sq - j_sq
        for h in range(RET_HEADS):
            lg = _ret_log_gamma(h)
            intra_ref[h] = jnp.where(dist >= 0, jnp.exp(lg * jnp.maximum(dist, 0.0)), 0.0) * scale
            qd_ref[h] = jnp.exp(lg * (i_dv + 1.0))
            kd_ref[h] = jnp.exp(lg * (BLOCK - 1.0 - i_sq)) * scale

    for c in range(chunks):
        rows = pl.ds(c * BLOCK, BLOCK)
        for h in range(RET_HEADS):
            s_decay = math.exp(_ret_log_gamma(h) * BLOCK)
            q = q_ref[rows, h * RET_DK:(h + 1) * RET_DK]
            k = k_ref[rows, h * RET_DK:(h + 1) * RET_DK]
            v = v_ref[rows, h * RET_DV:(h + 1) * RET_DV]
            st = state_ref[h]
            scores = lax.dot_general(q, k, (((1,), (1,)), ((), ())), preferred_element_type=F32)
            scores = (scores * intra_ref[h]).astype(BF16)
            out = (jnp.dot(scores, v, preferred_element_type=F32)
                   + qd_ref[h] * jnp.dot(q, st.astype(BF16), preferred_element_type=F32))
            kdec = (k.astype(F32) * kd_ref[h]).astype(BF16)
            state_ref[h] = st * s_decay + lax.dot_general(
                kdec, v, (((0,), (0,)), ((), ())), preferred_element_type=F32)
            mu = jnp.mean(out, axis=-1, keepdims=True)
            cen = out - mu
            var = jnp.mean(cen * cen, axis=-1, keepdims=True)
            g = g_ref[rows, h * RET_DV:(h + 1) * RET_DV].astype(F32)
            silu = g / (1.0 + jnp.exp(-g))
            o_ref[rows, h * RET_DV:(h + 1) * RET_DV] = (cen * lax.rsqrt(var + EPS) * silu).astype(o_ref.dtype)


def _retention(proj, batch, p_rows):
    n = proj.shape[0]
    nc = p_rows // BLOCK
    chunks = _largest_divisor(nc, (13, 5, 3, 1))
    steps = nc // chunks
    tr = chunks * BLOCK
    qk_w = RET_HEADS * RET_DK

    def rows(b, c):
        return b * steps + c

    return pl.pallas_call(
        functools.partial(_retention_kernel, chunks=chunks),
        grid=(batch, steps),
        in_specs=[
            pl.BlockSpec((tr, qk_w), lambda b, c: (rows(b, c), COL_RQ // qk_w)),
            pl.BlockSpec((tr, qk_w), lambda b, c: (rows(b, c), COL_RK // qk_w)),
            pl.BlockSpec((tr, D_MODEL), lambda b, c: (rows(b, c), COL_RV // D_MODEL)),
            pl.BlockSpec((tr, D_MODEL), lambda b, c: (rows(b, c), COL_RG // D_MODEL)),
        ],
        out_specs=pl.BlockSpec((tr, D_MODEL), lambda b, c: (rows(b, c), 0)),
        out_shape=jax.ShapeDtypeStruct((n, D_MODEL), BF16),
        scratch_shapes=[
            pltpu.VMEM((RET_HEADS, RET_DK, RET_DV), F32),
            pltpu.VMEM((RET_HEADS, BLOCK, BLOCK), F32),
            pltpu.VMEM((RET_HEADS, BLOCK, RET_DV), F32),
            pltpu.VMEM((RET_HEADS, BLOCK, BLOCK), F32),
        ],
        compiler_params=_params("parallel", "arbitrary"),
        name="retention",
    )(proj, proj, proj, proj)


FEAT_BLK, FEAT_ROW, FEAT_MASK, FEAT_Q0, FEAT_QROW = 0, 3, 6, 7, 10
FEAT_SPLIT = 3
LOG2E = math.log2(math.e)
VT_ROWS = DIFF_VD + BF16_SUBLANES
SKIP_LOG2_MARGIN = 160.0
SKIP_NORM_SLACK = 1.01


def _max_sq_norm(x, map_selector):
    sq = x.astype(F32)
    sq = (sq * sq).astype(BF16)
    per_map = jnp.dot(sq, map_selector, preferred_element_type=F32)
    return jnp.max(jnp.max(per_map, axis=0, keepdims=True), axis=-1, keepdims=True)


def _first_live_tile(q2, k2, head, *, tk, n_tiles):
    blocks_per_tile = tk // BLOCK
    qi = lax.broadcasted_iota(jnp.int32, (BLOCK, BLOCK), 0)
    tj = lax.broadcasted_iota(jnp.int32, (BLOCK, BLOCK), 1)
    diag_tile = lax.shift_right_logical(qi, blocks_per_tile.bit_length() - 1)
    slope2 = jnp.exp2(-(jnp.zeros((BLOCK, BLOCK), F32) + head + 1.0)) * LOG2E
    kd2 = jnp.max(jnp.where(tj == diag_tile, k2, 0.0), axis=-1, keepdims=True)
    dist_min = (qi * BLOCK - tj * tk - (tk - 1)).astype(F32)
    bound = SKIP_NORM_SLACK * jnp.sqrt(q2) * (jnp.sqrt(k2) + jnp.sqrt(kd2)) + SKIP_LOG2_MARGIN
    dead = jnp.where((slope2 * dist_min > bound) & (tj < diag_tile), 1.0, 0.0)
    run = jnp.ones((BLOCK, 1), F32)
    count = jnp.zeros((BLOCK, 1), F32)
    for j in range(n_tiles):
        run = run * dead[:, j:j + 1]
        count = count + run
    return count


def _kvprep_kernel(q_ref, k_ref, v_ref, k_out_ref, vt_ref, js_ref, *, tk, n_real, n_tiles):
    blocks_per_tile = tk // BLOCK
    hw = 2 * DIFF_HD
    lane = lax.broadcasted_iota(jnp.int32, (BLOCK, hw), 1)
    row = lax.broadcasted_iota(jnp.int32, (BLOCK, hw), 0)
    map_selector = jnp.where(((lane == 0) & (row < DIFF_HD)) | ((lane == 1) & (row >= DIFF_HD)), 1.0, 0.0)
    map_selector = map_selector.astype(BF16)
    ones_row = jnp.where(lax.broadcasted_iota(jnp.int32, (VT_ROWS - DIFF_VD, BLOCK), 0) == 0, 1.0, 0.0)
    q2 = jnp.zeros((BLOCK, BLOCK), F32)
    k2 = jnp.zeros((BLOCK, BLOCK), F32)
    for t in range(n_tiles):
        for bi in range(blocks_per_tile):
            blk = t * blocks_per_tile + bi
            rows = pl.ds(bi * BLOCK, BLOCK)
            cols = pl.ds(bi * BLOCK, BLOCK)
            vt_ref[0, t, DIFF_VD:, cols] = ones_row.astype(BF16)
            if blk < n_real:
                src = slice(blk * BLOCK, (blk + 1) * BLOCK)
                kb = k_ref[src, :]
                k_out_ref[0, t, rows, :] = kb
                vt_ref[0, t, 0:DIFF_VD, cols] = v_ref[src, :].astype(F32).T.astype(BF16)
                q2 = jnp.where(row == blk, _max_sq_norm(q_ref[src, :], map_selector), q2)
                k2 = jnp.where(lane == t, jnp.maximum(k2, _max_sq_norm(kb, map_selector)), k2)
            else:
                k_out_ref[0, t, rows, :] = jnp.zeros((BLOCK, hw), BF16)
                vt_ref[0, t, 0:DIFF_VD, cols] = jnp.zeros((DIFF_VD, BLOCK), BF16)
    first = _first_live_tile(q2, k2, pl.program_id(1).astype(F32), tk=tk, n_tiles=n_tiles)
    js_ref[0] = jnp.broadcast_to(first, (BLOCK, BLOCK)).astype(jnp.int32)


def _kvprep(proj, batch, p_rows, tk):
    n_real = p_rows // BLOCK
    n_tiles = -(-p_rows // tk)
    hw = 2 * DIFF_HD
    blocks_per_tile = tk // BLOCK
    assert hw == BLOCK and n_real <= BLOCK and blocks_per_tile & (blocks_per_tile - 1) == 0
    return pl.pallas_call(
        functools.partial(_kvprep_kernel, tk=tk, n_real=n_real, n_tiles=n_tiles),
        grid=(batch, DIFF_HEADS),
        in_specs=[
            pl.BlockSpec((p_rows, hw), lambda b, h: (b, COL_DQ // hw + h)),
            pl.BlockSpec((p_rows, hw), lambda b, h: (b, COL_DK // hw + h)),
            pl.BlockSpec((p_rows, DIFF_VD), lambda b, h: (b, COL_DV // DIFF_VD + h)),
        ],
        out_specs=[
            pl.BlockSpec((1, n_tiles, tk, hw), lambda b, h: (b * DIFF_HEADS + h, 0, 0, 0)),
            pl.BlockSpec((1, n_tiles, VT_ROWS, tk), lambda b, h: (b * DIFF_HEADS + h, 0, 0, 0)),
            pl.BlockSpec((1, BLOCK, BLOCK), lambda b, h: (b * DIFF_HEADS + h, 0, 0)),
        ],
        out_shape=[
            jax.ShapeDtypeStruct((batch * DIFF_HEADS, n_tiles, tk, hw), BF16),
            jax.ShapeDtypeStruct((batch * DIFF_HEADS, n_tiles, VT_ROWS, tk), BF16),
            jax.ShapeDtypeStruct((batch * DIFF_HEADS, BLOCK, BLOCK), jnp.int32),
        ],
        compiler_params=_params("parallel", "parallel"),
        name="kvprep",
    )(proj, proj, proj)


def _attn_kernel(first_ref, q_ref, qn_ref, k_ref, vt_ref, feat_ref, lamv_ref, lami_ref, sw_ref, o_ref,
                 s_ref, smax_ref, m_ref, acc_ref, qcat_ref, *, tk, heads, q_blocks):
    tq = BLOCK
    hp = pl.program_id(1)
    lanes = 2 * tq
    hw = 2 * DIFF_HD
    blocks_per_tile = tk // BLOCK
    n_q_total = pl.num_programs(2) * q_blocks
    head0 = pl.program_id(0) * DIFF_HEADS + hp * heads

    lane_h = lax.broadcasted_iota(jnp.int32, (tq, hw), 1)
    row_q = lax.broadcasted_iota(jnp.int32, (tq, hw), 0).astype(F32)
    lv = lamv_ref[...]
    lam = (jnp.exp(jnp.sum(lv[0:1] * lv[1:2], axis=-1, keepdims=True))
           - jnp.exp(jnp.sum(lv[2:3] * lv[3:4], axis=-1, keepdims=True)) + lami_ref[...])

    def split_into(feat, first_lane, x):
        for piece in range(FEAT_SPLIT):
            head_part = x.astype(BF16).astype(F32)
            feat = jnp.where(lane_h == first_lane + piece, head_part, feat)
            x = x - head_part
        return feat

    def tile_range(qi):
        n_full = lax.shift_right_logical(qi, blocks_per_tile.bit_length() - 1)
        j_first = n_full
        for hh in range(heads):
            j_first = jnp.minimum(j_first, first_ref[(head0 + hh) * n_q_total + qi])
        return n_full, j_first

    def slope_log2(hh, shape):
        head = (hp * heads + hh).astype(F32)
        return jnp.exp2(-(jnp.zeros(shape, F32) + head + 1.0)) * LOG2E

    def block_independent_coefficients(hh):
        slope2 = slope_log2(hh, (tq, hw))
        feat = jnp.where(lane_h == FEAT_MASK, NEG_INF, 0.0)
        feat = split_into(feat, FEAT_BLK, slope2 * BLOCK)
        feat = split_into(feat, FEAT_ROW, slope2)
        feat = split_into(feat, FEAT_QROW, -slope2 * row_q)
        return jnp.concatenate([feat.T, feat.T], axis=1)

    coefficient_row = lax.broadcasted_iota(jnp.int32, (hw, lanes), 0)
    static_coefficients = [block_independent_coefficients(hh) for hh in range(heads)]

    def qcat_of(hh, qi, q_block):
        bottom = static_coefficients[hh]
        x = -slope_log2(hh, (1, lanes)) * (qi * tq).astype(F32)
        for piece in range(FEAT_SPLIT):
            head_part = x.astype(BF16).astype(F32)
            bottom = jnp.where(coefficient_row == FEAT_Q0 + piece, head_part, bottom)
            x = x - head_part
        q = q_block[:, hh * hw:(hh + 1) * hw].astype(F32)
        q_m0 = jnp.where(lane_h < DIFF_HD, q, 0.0)
        q_m1 = jnp.where(lane_h >= DIFF_HD, q, 0.0)
        top = jnp.concatenate([q_m0.T, q_m1.T], axis=1)
        return jnp.concatenate([top, bottom], axis=0).astype(BF16)

    def score_tile(j, slot, qcats):
        for hh in range(heads):
            k_aug = jnp.concatenate([k_ref[hh, j], feat_ref[j]], axis=1)
            s = jnp.dot(k_aug, qcats[hh], preferred_element_type=F32)
            s_ref[slot, hh] = s
            smax_ref[slot, hh] = jnp.max(s, axis=0, keepdims=True)

    def start_block(qi, q_block):
        _, j_first = tile_range(qi)
        qcats = [qcat_of(hh, qi, q_block) for hh in range(heads)]
        for hh in range(heads):
            qcat_ref[hh] = qcats[hh]
        score_tile(j_first, 0, qcats)

    @pl.when(pl.program_id(2) == 0)
    def _():
        start_block(pl.program_id(2) * q_blocks, q_ref[0:tq, :])

    def query_block(sub, carry):
        qi = pl.program_id(2) * q_blocks + sub
        q_rows = pl.ds(pl.multiple_of(sub * tq, tq), tq)
        n_full, j_first = tile_range(qi)
        qcats = [qcat_ref[hh] for hh in range(heads)]

        def scores(j, slot):
            score_tile(j, slot, qcats)

        def accumulate(hh, s, s_max, vt):
            m_old = m_ref[hh]
            m_new = jnp.maximum(m_old, s_max)
            alpha = jnp.exp2(m_old - m_new)
            p = jnp.exp2(s - m_new).astype(BF16)
            m_ref[hh] = m_new
            pv = jnp.dot(vt, p, preferred_element_type=F32)
            acc_ref[hh] = acc_ref[hh] * alpha + pv

        def update(j, slot):
            for hh in range(heads):
                accumulate(hh, s_ref[slot, hh], smax_ref[slot, hh], vt_ref[hh, j])

        def update_diagonal(j, slot, diag_block):
            live = (diag_block + 1) * BLOCK
            k_in_block = lax.broadcasted_iota(jnp.int32, (BLOCK, lanes), 0)
            q_in_block = lax.broadcasted_iota(jnp.int32, (BLOCK, lanes), 1)
            q_in_block = jnp.where(q_in_block >= tq, q_in_block - tq, q_in_block)
            for hh in range(heads):
                s = s_ref[slot, hh, 0:live, :]
                s_diag = jnp.where(k_in_block <= q_in_block, s[live - BLOCK:live], NEG_INF)
                if diag_block > 0:
                    s = jnp.concatenate([s[0:live - BLOCK], s_diag], axis=0)
                else:
                    s = s_diag
                accumulate(hh, s, jnp.max(s, axis=0, keepdims=True), vt_ref[hh, j, :, 0:live])
                o = acc_ref[hh, 0:DIFF_VD] / acc_ref[hh, DIFF_VD:DIFF_VD + 1]
                dlt = o[:, 0:tq] - lam * o[:, tq:lanes]
                ms = jnp.mean(dlt * dlt, axis=0, keepdims=True)
                y = dlt * lax.rsqrt(ms + EPS) * sw_ref[...] * (1.0 - lami_ref[...])
                o_ref[q_rows, hh * DIFF_VD:(hh + 1) * DIFF_VD] = y.T.astype(o_ref.dtype)
            next_rows = pl.ds(pl.multiple_of(jnp.minimum(sub + 1, q_blocks - 1) * tq, tq), tq)
            q_next = jnp.where(sub == q_blocks - 1, qn_ref[...], q_ref[next_rows, :])
            start_block(jnp.minimum(qi + 1, n_q_total - 1), q_next)

        m_ref[...] = jnp.full_like(m_ref, NEG_INF)
        acc_ref[...] = jnp.zeros_like(acc_ref)


        def tiles(j, count):
            for t in range(count):
                scores(j + t + 1, (t + 1) % 2)
                update(j + t, t % 2)

        n_loop = n_full - j_first
        j = j_first
        count = ATTN_MAX_TRIP_TILES
        while count >= 1:
            def body(i, c, base=j, count=count):
                tiles(base + count * i, count)
                return c
            if count == ATTN_MAX_TRIP_TILES:
                trips = lax.shift_right_logical(n_loop, count.bit_length() - 1)
            else:
                trips = lax.shift_right_logical(n_loop & count, count.bit_length() - 1)
            lax.fori_loop(0, trips, body, 0)
            j = j + count * trips
            count //= 2
        diag_block = qi & (blocks_per_tile - 1)
        diag_slot = n_loop & 1
        for c in range(blocks_per_tile):
            pl.when(diag_block == c)(functools.partial(update_diagonal, n_full, diag_slot, c))
        return carry

    lax.fori_loop(0, q_blocks, query_block, 0)


def _position_features(n_tiles, tk, p_rows):
    kpos = jnp.arange(n_tiles * tk, dtype=jnp.int32)[:, None]
    lane = jnp.arange(2 * DIFF_HD, dtype=jnp.int32)[None, :]
    masked = (kpos < META_PAD) | (kpos >= p_rows)
    feat = jnp.where(lane < FEAT_ROW, kpos // BLOCK,
           jnp.where(lane < FEAT_MASK, kpos % BLOCK,
           jnp.where(lane == FEAT_MASK, masked.astype(jnp.int32),
           jnp.where(lane < FEAT_QROW + FEAT_SPLIT, 1, 0))))
    return feat.astype(BF16).reshape(n_tiles, tk, 2 * DIFF_HD)


def _diff_attention(proj, lam_vecs, lam_init, subln_w, batch, p_rows):
    n = proj.shape[0]
    tq = BLOCK
    tk = ATTN_KEY_TILE
    heads = ATTN_HEADS_PER_STEP
    nq = p_rows // tq
    q_blocks = _largest_divisor(nq, (13, 5, 3, 1))
    steps = nq // q_blocks
    tr = q_blocks * tq
    hw = 2 * DIFF_HD
    k_tiles, v_t, first_tile = _kvprep(proj, batch, p_rows, tk)
    first_tile = first_tile[:, :nq, 0].reshape(batch * DIFF_HEADS * nq)
    n_tiles = k_tiles.shape[1]
    feat = _position_features(n_tiles, tk, p_rows)
    groups = DIFF_HEADS // heads
    grid_spec = pltpu.PrefetchScalarGridSpec(
        num_scalar_prefetch=1,
        grid=(batch, groups, steps),
        in_specs=[
            pl.BlockSpec((tr, heads * hw), lambda b, g, i, first: (b * steps + i, COL_DQ // (heads * hw) + g)),
            pl.BlockSpec((tq, heads * hw), lambda b, g, i, first: (
                b * nq + jnp.minimum((i + 1) * q_blocks, nq - 1), COL_DQ // (heads * hw) + g)),
            pl.BlockSpec((heads, n_tiles, tk, hw), lambda b, g, i, first: (b * groups + g, 0, 0, 0)),
            pl.BlockSpec((heads, n_tiles, VT_ROWS, tk), lambda b, g, i, first: (b * groups + g, 0, 0, 0)),
            pl.BlockSpec((n_tiles, tk, hw), lambda b, g, i, first: (0, 0, 0), pipeline_mode=pl.Buffered(1)),
            pl.BlockSpec((4, DIFF_HD), lambda b, g, i, first: (0, 0)),
            pl.BlockSpec((1, 1), lambda b, g, i, first: (0, 0)),
            pl.BlockSpec((DIFF_VD, 1), lambda b, g, i, first: (0, 0)),
        ],
        out_specs=pl.BlockSpec((tr, heads * DIFF_VD), lambda b, g, i, first: (b * steps + i, g)),
        scratch_shapes=[
            pltpu.VMEM((2, heads, tk, 2 * tq), F32),
            pltpu.VMEM((2, heads, 1, 2 * tq), F32),
            pltpu.VMEM((heads, 1, 2 * tq), F32),
            pltpu.VMEM((heads, VT_ROWS, 2 * tq), F32),
            pltpu.VMEM((heads, 2 * hw, 2 * tq), BF16),
        ],
    )
    return pl.pallas_call(
        functools.partial(_attn_kernel, tk=tk, heads=heads, q_blocks=q_blocks),
        grid_spec=grid_spec,
        out_shape=jax.ShapeDtypeStruct((n, D_MODEL), BF16),
        compiler_params=_params("arbitrary", "arbitrary", "arbitrary"),
        name="diffattn",
    )(first_tile, proj, proj, k_tiles, v_t, feat, lam_vecs, lam_init, subln_w)


def _merge_kernel(ret_ref, diff_ref, cb_ref, cc_ref, cx_ref, cch_ref, cxh_ref, g_ref, x_ref,
                  cw_ref, wb_ref, wo_ref, o_ref, *, tm, p_rows):
    start = pl.program_id(0) * tm
    rowi = lax.broadcasted_iota(jnp.int32, (tm, D_MODEL), 0)

    u = cc_ref[...].astype(F32) * cx_ref[...].astype(F32)
    uh = cch_ref[...].astype(F32) * cxh_ref[...].astype(F32)
    h1 = uh[BF16_SUBLANES - 1:BF16_SUBLANES, :]
    h2 = uh[BF16_SUBLANES - 2:BF16_SUBLANES - 1, :]
    u1 = jnp.where(rowi == 0, h1, pltpu.roll(u, 1, axis=0))
    u2 = jnp.where(rowi == 0, h2, jnp.where(rowi == 1, h1, pltpu.roll(u, 2, axis=0)))
    cw = cw_ref[...]
    conv = cb_ref[...].astype(F32) * (cw[0:1] * u2 + cw[1:2] * u1 + cw[2:3] * u)

    g = g_ref[...].astype(F32)
    g = 1.0 / (1.0 + jnp.exp(-g))
    merged = (g[:, 0:D_MODEL] * jnp.dot(ret_ref[...], wb_ref[0], preferred_element_type=F32)
              + g[:, D_MODEL:2 * D_MODEL] * jnp.dot(diff_ref[...], wb_ref[1], preferred_element_type=F32)
              + g[:, 2 * D_MODEL:] * jnp.dot(conv.astype(BF16), wb_ref[2], preferred_element_type=F32))
    y = x_ref[...] + jnp.dot(merged.astype(BF16), wo_ref[...], preferred_element_type=F32)
    r = rowi + (start - (start // p_rows) * p_rows)
    is_pad = (r < META_PAD) | ((r >= p_rows) & (r < p_rows + META_PAD))
    o_ref[...] = jnp.where(is_pad, 0.0, y)


def _merge(ret, diff, proj, x, conv_w, w_branch, w_out, p_rows):
    n, d = x.shape
    tm = _largest_divisor(n, (512, 256, 128))
    halo_blocks = tm // BF16_SUBLANES

    def col(c):
        return lambda i: (i, c // d)

    def halo(c):
        return lambda i: (jnp.maximum(i * halo_blocks - 1, 0), c // d)

    return pl.pallas_call(
        functools.partial(_merge_kernel, tm=tm, p_rows=p_rows),
        grid=(n // tm,),
        in_specs=[
            pl.BlockSpec((tm, d), lambda i: (i, 0)),
            pl.BlockSpec((tm, d), lambda i: (i, 0)),
            pl.BlockSpec((tm, d), col(COL_CB)),
            pl.BlockSpec((tm, d), col(COL_CC)),
            pl.BlockSpec((tm, d), col(COL_CX)),
            pl.BlockSpec((BF16_SUBLANES, d), halo(COL_CC)),
            pl.BlockSpec((BF16_SUBLANES, d), halo(COL_CX)),
            pl.BlockSpec((tm, N_BRANCH * d), lambda i: (i, COL_GATE // (N_BRANCH * d))),
            pl.BlockSpec((tm, d), lambda i: (i, 0)),
            pl.BlockSpec((CONV_K, d), lambda i: (0, 0)),
            pl.BlockSpec((N_BRANCH, d, d), lambda i: (0, 0, 0), pipeline_mode=pl.Buffered(1)),
            pl.BlockSpec((d, d), lambda i: (0, 0), pipeline_mode=pl.Buffered(1)),
        ],
        out_specs=pl.BlockSpec((tm, d), lambda i: (i, 0)),
        out_shape=jax.ShapeDtypeStruct((n, d), F32),
        compiler_params=_params("parallel"),
        name="merge",
    )(ret, diff, proj, proj, proj, proj, proj, proj, x, conv_w, w_branch, w_out)


def _mlp_kernel(x_ref, nw_ref, wu_ref, wd_ref, o_ref, *, ff_chunk):
    x = x_ref[...]
    xn = _rms(x, nw_ref[...]).astype(BF16)
    acc = x
    for c in range(wu_ref.shape[1] // ff_chunk):
        cols = slice(c * ff_chunk, (c + 1) * ff_chunk)
        u = jnp.maximum(jnp.dot(xn, wu_ref[:, cols], preferred_element_type=F32), 0.0)
        acc = acc + jnp.dot((u * u).astype(BF16), wd_ref[cols, :], preferred_element_type=F32)
    o_ref[...] = acc


def _mlp(x, norm_w, w_up, w_down):
    n, d = x.shape
    d_ff = w_up.shape[1]
    tm = _largest_divisor(n, (512, 256, 128))
    return pl.pallas_call(
        functools.partial(_mlp_kernel, ff_chunk=1024),
        grid=(n // tm,),
        in_specs=[
            pl.BlockSpec((tm, d), lambda i: (i, 0)),
            pl.BlockSpec((1, d), lambda i: (0, 0)),
            pl.BlockSpec((d, d_ff), lambda i: (0, 0)),
            pl.BlockSpec((d_ff, d), lambda i: (0, 0)),
        ],
        out_specs=pl.BlockSpec((tm, d), lambda i: (i, 0)),
        out_shape=jax.ShapeDtypeStruct((n, d), F32),
        compiler_params=_params("parallel"),
        name="mlp",
    )(x, norm_w, w_up, w_down)


def _final_norm_kernel(x_ref, w_ref, o_ref):
    o_ref[...] = _rms(x_ref[...], w_ref[...])


def _final_norm(h, w, batch, seq, p_rows):
    d = h.shape[1]
    tr = _largest_divisor(seq, (1024, 512, 256, 128))
    steps = seq // tr
    out = pl.pallas_call(
        _final_norm_kernel,
        grid=(batch, steps),
        in_specs=[
            pl.BlockSpec((pl.Element(tr), pl.Element(d)),
                         lambda b, i: (pl.multiple_of(b * p_rows + BLOCK + i * tr, BLOCK), 0)),
            pl.BlockSpec((1, d), lambda b, i: (0, 0)),
        ],
        out_specs=pl.BlockSpec((tr, d), lambda b, i: (b * steps + i, 0)),
        out_shape=jax.ShapeDtypeStruct((batch * seq, d), h.dtype),
        compiler_params=_params("parallel", "parallel"),
        name="final_norm",
    )(h, w)
    return out.reshape(batch, seq, d)


def kernel(x, meta_tokens, norm1_w, w_in, conv_w, diff_lambda, diff_subln_w, w_branch, w_out,
           norm2_w, w_up, w_down, final_norm_w):
    batch, seq, d = x.shape
    depth = w_in.shape[0]
    assert d == D_MODEL and w_in.shape[1:] == (D_MODEL, D_IN) and seq % BLOCK == 0
    p_rows = META_PAD + N_META + seq

    meta = jnp.broadcast_to(meta_tokens.astype(x.dtype)[None], (batch, N_META, d))
    h = jnp.concatenate([jnp.zeros((batch, META_PAD, d), x.dtype), meta, x], axis=1).reshape(batch * p_rows, d)

    layer_ids = jnp.arange(depth, dtype=F32)
    lam_inits = (0.8 - 0.6 * jnp.exp(-0.3 * layer_ids)).reshape(depth, 1, 1)
    w_in = w_in.at[:, :, COL_DQ:COL_DK].multiply(DIFF_HD ** -0.5 * LOG2E)
    layers = dict(
        norm1_w=norm1_w.reshape(depth, 1, d), w_in=w_in.astype(BF16), conv_w=conv_w,
        lam_vecs=diff_lambda.astype(F32), lam_init=lam_inits, subln_w=diff_subln_w.reshape(depth, DIFF_VD, 1),
        w_branch=w_branch.astype(BF16), w_out=w_out.astype(BF16), norm2_w=norm2_w.reshape(depth, 1, d),
        w_up=w_up.astype(BF16), w_down=w_down.astype(BF16))

    def layer(h, p):
        proj = _inproj(h, p["norm1_w"], p["w_in"])
        ret = _retention(proj, batch, p_rows)
        diff = _diff_attention(proj, p["lam_vecs"], p["lam_init"], p["subln_w"], batch, p_rows)
        h = _merge(ret, diff, proj, h, p["conv_w"], p["w_branch"], p["w_out"], p_rows)
        h = _mlp(h, p["norm2_w"], p["w_up"], p["w_down"])
        return h, None

    for index in range(depth):
        h, _ = layer(h, {name: stacked[index] for name, stacked in layers.items()})
    return _final_norm(h, final_norm_w.reshape(1, d), batch, seq, p_rows)
```

```python
import functools
import math

import jax
import jax.numpy as jnp
from jax import lax
from jax.experimental import pallas as pl
from jax.experimental.pallas import tpu as pltpu

N_META = 16
BLOCK = 128
META_PAD = BLOCK - N_META
RET_HEADS = 4
RET_DK = 128
RET_DV = 256
DIFF_HEADS = 8
DIFF_HD = 64
DIFF_VD = 128
CONV_K = 3
N_BRANCH = 3
EPS = 1e-6
NEG_INF = -1e30

D_MODEL = RET_HEADS * RET_DV
COL_RQ, COL_RK, COL_RV, COL_RG = 0, 512, 1024, 2048
COL_DQ, COL_DK, COL_DV = 3072, 4096, 5120
COL_CB, COL_CC, COL_CX, COL_GATE = 6144, 7168, 8192, 9216
D_IN = COL_GATE + N_BRANCH * D_MODEL

V7X_VMEM_LIMIT_BYTES = 56 * 1024 * 1024
BF16_SUBLANES = 16
ATTN_KEY_TILE = 512
ATTN_MAX_TRIP_TILES = 4
ATTN_HEADS_PER_STEP = 4

F32 = jnp.float32
BF16 = jnp.bfloat16


def _largest_divisor(n, candidates):
    for c in candidates:
        if n % c == 0:
            return c
    raise ValueError(f"no tile in {candidates} divides {n}")


def _params(*semantics):
    return pltpu.CompilerParams(dimension_semantics=semantics, vmem_limit_bytes=V7X_VMEM_LIMIT_BYTES)


def _rms(x, w):
    return x * lax.rsqrt(jnp.mean(x * x, axis=-1, keepdims=True) + EPS) * w


def _inproj_kernel(x_ref, nw_ref, w_ref, o_ref, xn_ref):
    @pl.when(pl.program_id(1) == 0)
    def _():
        xn_ref[...] = _rms(x_ref[...], nw_ref[...]).astype(BF16)

    o_ref[...] = jnp.dot(xn_ref[...], w_ref[...], preferred_element_type=F32).astype(o_ref.dtype)


def _inproj(h, norm_w, w_in, layer):
    n, d = h.shape
    tm = _largest_divisor(n, (1024, 512, 256, 128))
    tn = 4096
    return pl.pallas_call(
        _inproj_kernel,
        grid=(n // tm, D_IN // tn),
        in_specs=[
            pl.BlockSpec((tm, d), lambda i, j: (i, 0)),
            pl.BlockSpec((1, d), lambda i, j: (0, 0)),
            pl.BlockSpec((None, d, tn), lambda i, j: (layer, 0, j)),
        ],
        out_specs=pl.BlockSpec((tm, tn), lambda i, j: (i, j)),
        out_shape=jax.ShapeDtypeStruct((n, D_IN), BF16),
        scratch_shapes=[pltpu.VMEM((tm, d), BF16)],
        compiler_params=_params("parallel", "arbitrary"),
        name="inproj",
    )(h, norm_w, w_in)


def _ret_log_gamma(head):
    return math.log1p(-(2.0 ** (-5.0 - head)))


def _retention_kernel(q_ref, k_ref, v_ref, g_ref, o_ref, state_ref, intra_ref, qd_ref, kd_ref, *, chunks):
    @pl.when(pl.program_id(1) == 0)
    def _():
        state_ref[...] = jnp.zeros_like(state_ref)
        scale = RET_DK ** -0.5
        i_sq = lax.broadcasted_iota(jnp.int32, (BLOCK, BLOCK), 0).astype(F32)
        j_sq = lax.broadcasted_iota(jnp.int32, (BLOCK, BLOCK), 1).astype(F32)
        i_dv = lax.broadcasted_iota(jnp.int32, (BLOCK, RET_DV), 0).astype(F32)
        dist = i_sq - j_sq
        for h in range(RET_HEADS):
            lg = _ret_log_gamma(h)
            intra_ref[h] = jnp.where(dist >= 0, jnp.exp(lg * jnp.maximum(dist, 0.0)), 0.0) * scale
            qd_ref[h] = jnp.exp(lg * (i_dv + 1.0))
            kd_ref[h] = jnp.exp(lg * (BLOCK - 1.0 - i_sq)) * scale

    for c in range(chunks):
        rows = pl.ds(c * BLOCK, BLOCK)
        for h in range(RET_HEADS):
            s_decay = math.exp(_ret_log_gamma(h) * BLOCK)
            q = q_ref[rows, h * RET_DK:(h + 1) * RET_DK]
            k = k_ref[rows, h * RET_DK:(h + 1) * RET_DK]
            v = v_ref[rows, h * RET_DV:(h + 1) * RET_DV]
            st = state_ref[h]
            scores = lax.dot_general(q, k, (((1,), (1,)), ((), ())), preferred_element_type=F32)
            scores = (scores * intra_ref[h]).astype(BF16)
            out = (jnp.dot(scores, v, preferred_element_type=F32)
                   + qd_ref[h] * jnp.dot(q, st.astype(BF16), preferred_element_type=F32))
            kdec = (k.astype(F32) * kd_ref[h]).astype(BF16)
            state_ref[h] = st * s_decay + lax.dot_general(
                kdec, v, (((0,), (0,)), ((), ())), preferred_element_type=F32)
            mu = jnp.mean(out, axis=-1, keepdims=True)
            cen = out - mu
            var = jnp.mean(cen * cen, axis=-1, keepdims=True)
            g = g_ref[rows, h * RET_DV:(h + 1) * RET_DV].astype(F32)
            silu = g / (1.0 + jnp.exp(-g))
            o_ref[rows, h * RET_DV:(h + 1) * RET_DV] = (cen * lax.rsqrt(var + EPS) * silu).astype(o_ref.dtype)


def _retention(proj, batch, p_rows):
    n = proj.shape[0]
    nc = p_rows // BLOCK
    chunks = _largest_divisor(nc, (13, 5, 3, 1))
    steps = nc // chunks
    tr = chunks * BLOCK
    qk_w = RET_HEADS * RET_DK

    def rows(b, c):
        return b * steps + c

    return pl.pallas_call(
        functools.partial(_retention_kernel, chunks=chunks),
        grid=(batch, steps),
        in_specs=[
            pl.BlockSpec((tr, qk_w), lambda b, c: (rows(b, c), COL_RQ // qk_w)),
            pl.BlockSpec((tr, qk_w), lambda b, c: (rows(b, c), COL_RK // qk_w)),
            pl.BlockSpec((tr, D_MODEL), lambda b, c: (rows(b, c), COL_RV // D_MODEL)),
            pl.BlockSpec((tr, D_MODEL), lambda b, c: (rows(b, c), COL_RG // D_MODEL)),
        ],
        out_specs=pl.BlockSpec((tr, D_MODEL), lambda b, c: (rows(b, c), 0)),
        out_shape=jax.ShapeDtypeStruct((n, D_MODEL), BF16),
        scratch_shapes=[
            pltpu.VMEM((RET_HEADS, RET_DK, RET_DV), F32),
            pltpu.VMEM((RET_HEADS, BLOCK, BLOCK), F32),
            pltpu.VMEM((RET_HEADS, BLOCK, RET_DV), F32),
            pltpu.VMEM((RET_HEADS, BLOCK, BLOCK), F32),
        ],
        compiler_params=_params("parallel", "arbitrary"),
        name="retention",
    )(proj, proj, proj, proj)


FEAT_BLK, FEAT_ROW, FEAT_MASK, FEAT_Q0, FEAT_QROW = 0, 3, 6, 7, 10
FEAT_SPLIT = 3
LOG2E = math.log2(math.e)
VT_ROWS = DIFF_VD + BF16_SUBLANES
SKIP_LOG2_MARGIN = 160.0
SKIP_NORM_SLACK = 1.01


def _max_sq_norm(x, map_selector):
    sq = x.astype(F32)
    sq = (sq * sq).astype(BF16)
    per_map = jnp.dot(sq, map_selector, preferred_element_type=F32)
    return jnp.max(jnp.max(per_map, axis=0, keepdims=True), axis=-1, keepdims=True)


def _first_live_tile(q2, k2, head, *, tk, n_tiles):
    blocks_per_tile = tk // BLOCK
    qi = lax.broadcasted_iota(jnp.int32, (BLOCK, BLOCK), 0)
    tj = lax.broadcasted_iota(jnp.int32, (BLOCK, BLOCK), 1)
    diag_tile = lax.shift_right_logical(qi, blocks_per_tile.bit_length() - 1)
    slope2 = jnp.exp2(-(jnp.zeros((BLOCK, BLOCK), F32) + head + 1.0)) * LOG2E
    kd2 = jnp.max(jnp.where(tj == diag_tile, k2, 0.0), axis=-1, keepdims=True)
    dist_min = (qi * BLOCK - tj * tk - (tk - 1)).astype(F32)
    bound = SKIP_NORM_SLACK * jnp.sqrt(q2) * (jnp.sqrt(k2) + jnp.sqrt(kd2)) + SKIP_LOG2_MARGIN
    dead = jnp.where((slope2 * dist_min > bound) & (tj < diag_tile), 1.0, 0.0)
    run = jnp.ones((BLOCK, 1), F32)
    count = jnp.zeros((BLOCK, 1), F32)
    for j in range(n_tiles):
        run = run * dead[:, j:j + 1]
        count = count + run
    return count


def _kvprep_kernel(q_ref, k_ref, v_ref, k_out_ref, vt_ref, js_ref, *, tk, n_real, n_tiles):
    blocks_per_tile = tk // BLOCK
    hw = 2 * DIFF_HD
    lane = lax.broadcasted_iota(jnp.int32, (BLOCK, hw), 1)
    row = lax.broadcasted_iota(jnp.int32, (BLOCK, hw), 0)
    map_selector = jnp.where(((lane == 0) & (row < DIFF_HD)) | ((lane == 1) & (row >= DIFF_HD)), 1.0, 0.0)
    map_selector = map_selector.astype(BF16)
    ones_row = jnp.where(lax.broadcasted_iota(jnp.int32, (VT_ROWS - DIFF_VD, BLOCK), 0) == 0, 1.0, 0.0)
    q2 = jnp.zeros((BLOCK, BLOCK), F32)
    k2 = jnp.zeros((BLOCK, BLOCK), F32)
    for t in range(n_tiles):
        for bi in range(blocks_per_tile):
            blk = t * blocks_per_tile + bi
            rows = pl.ds(bi * BLOCK, BLOCK)
            cols = pl.ds(bi * BLOCK, BLOCK)
            vt_ref[0, t, DIFF_VD:, cols] = ones_row.astype(BF16)
            if blk < n_real:
                src = slice(blk * BLOCK, (blk + 1) * BLOCK)
                kb = k_ref[src, :]
                k_out_ref[0, t, rows, :] = kb
                vt_ref[0, t, 0:DIFF_VD, cols] = v_ref[src, :].astype(F32).T.astype(BF16)
                q2 = jnp.where(row == blk, _max_sq_norm(q_ref[src, :], map_selector), q2)
                k2 = jnp.where(lane == t, jnp.maximum(k2, _max_sq_norm(kb, map_selector)), k2)
            else:
                k_out_ref[0, t, rows, :] = jnp.zeros((BLOCK, hw), BF16)
                vt_ref[0, t, 0:DIFF_VD, cols] = jnp.zeros((DIFF_VD, BLOCK), BF16)
    first = _first_live_tile(q2, k2, pl.program_id(1).astype(F32), tk=tk, n_tiles=n_tiles)
    js_ref[0] = jnp.broadcast_to(first, (BLOCK, BLOCK)).astype(jnp.int32)


def _kvprep(proj, batch, p_rows, tk):
    n_real = p_rows // BLOCK
    n_tiles = -(-p_rows // tk)
    hw = 2 * DIFF_HD
    blocks_per_tile = tk // BLOCK
    assert hw == BLOCK and n_real <= BLOCK and blocks_per_tile & (blocks_per_tile - 1) == 0
    return pl.pallas_call(
        functools.partial(_kvprep_kernel, tk=tk, n_real=n_real, n_tiles=n_tiles),
        grid=(batch, DIFF_HEADS),
        in_specs=[
            pl.BlockSpec((p_rows, hw), lambda b, h: (b, COL_DQ // hw + h)),
            pl.BlockSpec((p_rows, hw), lambda b, h: (b, COL_DK // hw + h)),
            pl.BlockSpec((p_rows, DIFF_VD), lambda b, h: (b, COL_DV // DIFF_VD + h)),
        ],
        out_specs=[
            pl.BlockSpec((1, n_tiles, tk, hw), lambda b, h: (b * DIFF_HEADS + h, 0, 0, 0)),
            pl.BlockSpec((1, n_tiles, VT_ROWS, tk), lambda b, h: (b * DIFF_HEADS + h, 0, 0, 0)),
            pl.BlockSpec((1, BLOCK, BLOCK), lambda b, h: (b * DIFF_HEADS + h, 0, 0)),
        ],
        out_shape=[
            jax.ShapeDtypeStruct((batch * DIFF_HEADS, n_tiles, tk, hw), BF16),
            jax.ShapeDtypeStruct((batch * DIFF_HEADS, n_tiles, VT_ROWS, tk), BF16),
            jax.ShapeDtypeStruct((batch * DIFF_HEADS, BLOCK, BLOCK), jnp.int32),
        ],
        compiler_params=_params("parallel", "parallel"),
        name="kvprep",
    )(proj, proj, proj)


def _attn_kernel(first_ref, q_ref, qn_ref, k_ref, vt_ref, feat_ref, lamv_ref, lami_ref, sw_ref, o_ref,
                 s_ref, smax_ref, m_ref, acc_ref, qcat_ref, *, tk, heads, q_blocks):
    tq = BLOCK
    hp = pl.program_id(1)
    lanes = 2 * tq
    hw = 2 * DIFF_HD
    blocks_per_tile = tk // BLOCK
    n_q_total = pl.num_programs(2) * q_blocks
    head0 = pl.program_id(0) * DIFF_HEADS + hp * heads

    lane_h = lax.broadcasted_iota(jnp.int32, (tq, hw), 1)
    row_q = lax.broadcasted_iota(jnp.int32, (tq, hw), 0).astype(F32)
    lv = lamv_ref[...]
    lam = (jnp.exp(jnp.sum(lv[0:1] * lv[1:2], axis=-1, keepdims=True))
           - jnp.exp(jnp.sum(lv[2:3] * lv[3:4], axis=-1, keepdims=True)) + lami_ref[...])

    def split_into(feat, first_lane, x):
        for piece in range(FEAT_SPLIT):
            head_part = x.astype(BF16).astype(F32)
            feat = jnp.where(lane_h == first_lane + piece, head_part, feat)
            x = x - head_part
        return feat

    def tile_range(qi):
        n_full = lax.shift_right_logical(qi, blocks_per_tile.bit_length() - 1)
        j_first = n_full
        for hh in range(heads):
            j_first = jnp.minimum(j_first, first_ref[(head0 + hh) * n_q_total + qi])
        return n_full, j_first

    def slope_log2(hh, shape):
        head = (hp * heads + hh).astype(F32)
        return jnp.exp2(-(jnp.zeros(shape, F32) + head + 1.0)) * LOG2E

    def block_independent_coefficients(hh):
        slope2 = slope_log2(hh, (tq, hw))
        feat = jnp.where(lane_h == FEAT_MASK, NEG_INF, 0.0)
        feat = split_into(feat, FEAT_BLK, slope2 * BLOCK)
        feat = split_into(feat, FEAT_ROW, slope2)
        feat = split_into(feat, FEAT_QROW, -slope2 * row_q)
        return jnp.concatenate([feat.T, feat.T], axis=1)

    coefficient_row = lax.broadcasted_iota(jnp.int32, (hw, lanes), 0)
    static_coefficients = [block_independent_coefficients(hh) for hh in range(heads)]

    def qcat_of(hh, qi, q_block):
        bottom = static_coefficients[hh]
        x = -slope_log2(hh, (1, lanes)) * (qi * tq).astype(F32)
        for piece in range(FEAT_SPLIT):
            head_part = x.astype(BF16).astype(F32)
            bottom = jnp.where(coefficient_row == FEAT_Q0 + piece, head_part, bottom)
            x = x - head_part
        q = q_block[:, hh * hw:(hh + 1) * hw].astype(F32)
        q_m0 = jnp.where(lane_h < DIFF_HD, q, 0.0)
        q_m1 = jnp.where(lane_h >= DIFF_HD, q, 0.0)
        top = jnp.concatenate([q_m0.T, q_m1.T], axis=1)
        return jnp.concatenate([top, bottom], axis=0).astype(BF16)

    def score_tile(j, slot, qcats):
        for hh in range(heads):
            k_aug = jnp.concatenate([k_ref[hh, j], feat_ref[j]], axis=1)
            s = jnp.dot(k_aug, qcats[hh], preferred_element_type=F32)
            s_ref[slot, hh] = s
            smax_ref[slot, hh] = jnp.max(s, axis=0, keepdims=True)

    def start_block(qi, q_block):
        _, j_first = tile_range(qi)
        qcats = [qcat_of(hh, qi, q_block) for hh in range(heads)]
        for hh in range(heads):
            qcat_ref[hh] = qcats[hh]
        score_tile(j_first, 0, qcats)

    @pl.when(pl.program_id(2) == 0)
    def _():
        start_block(pl.program_id(2) * q_blocks, q_ref[0:tq, :])

    def query_block(sub, carry):
        qi = pl.program_id(2) * q_blocks + sub
        q_rows = pl.ds(pl.multiple_of(sub * tq, tq), tq)
        n_full, j_first = tile_range(qi)
        qcats = [qcat_ref[hh] for hh in range(heads)]

        def scores(j, slot):
            score_tile(j, slot, qcats)

        def accumulate(hh, s, s_max, vt):
            m_old = m_ref[hh]
            m_new = jnp.maximum(m_old, s_max)
            alpha = jnp.exp2(m_old - m_new)
            p = jnp.exp2(s - m_new).astype(BF16)
            m_ref[hh] = m_new
            pv = jnp.dot(vt, p, preferred_element_type=F32)
            acc_ref[hh] = acc_ref[hh] * alpha + pv

        def update(j, slot):
            for hh in range(heads):
                accumulate(hh, s_ref[slot, hh], smax_ref[slot, hh], vt_ref[hh, j])

        def update_diagonal(j, slot, diag_block):
            live = (diag_block + 1) * BLOCK
            k_in_block = lax.broadcasted_iota(jnp.int32, (BLOCK, lanes), 0)
            q_in_block = lax.broadcasted_iota(jnp.int32, (BLOCK, lanes), 1)
            q_in_block = jnp.where(q_in_block >= tq, q_in_block - tq, q_in_block)
            for hh in range(heads):
                s = s_ref[slot, hh, 0:live, :]
                s_diag = jnp.where(k_in_block <= q_in_block, s[live - BLOCK:live], NEG_INF)
                if diag_block > 0:
                    s = jnp.concatenate([s[0:live - BLOCK], s_diag], axis=0)
                else:
                    s = s_diag
                accumulate(hh, s, jnp.max(s, axis=0, keepdims=True), vt_ref[hh, j, :, 0:live])
                o = acc_ref[hh, 0:DIFF_VD] / acc_ref[hh, DIFF_VD:DIFF_VD + 1]
                dlt = o[:, 0:tq] - lam * o[:, tq:lanes]
                ms = jnp.mean(dlt * dlt, axis=0, keepdims=True)
                y = dlt * lax.rsqrt(ms + EPS) * sw_ref[...] * (1.0 - lami_ref[...])
                o_ref[q_rows, hh * DIFF_VD:(hh + 1) * DIFF_VD] = y.T.astype(o_ref.dtype)
            next_rows = pl.ds(pl.multiple_of(jnp.minimum(sub + 1, q_blocks - 1) * tq, tq), tq)
            q_next = jnp.where(sub == q_blocks - 1, qn_ref[...], q_ref[next_rows, :])
            start_block(jnp.minimum(qi + 1, n_q_total - 1), q_next)

        m_ref[...] = jnp.full_like(m_ref, NEG_INF)
        acc_ref[...] = jnp.zeros_like(acc_ref)


        def tiles(j, count):
            for t in range(count):
                scores(j + t + 1, (t + 1) % 2)
                update(j + t, t % 2)

        n_loop = n_full - j_first
        j = j_first
        count = ATTN_MAX_TRIP_TILES
        while count >= 1:
            def body(i, c, base=j, count=count):
                tiles(base + count * i, count)
                return c
            if count == ATTN_MAX_TRIP_TILES:
                trips = lax.shift_right_logical(n_loop, count.bit_length() - 1)
            else:
                trips = lax.shift_right_logical(n_loop & count, count.bit_length() - 1)
            lax.fori_loop(0, trips, body, 0)
            j = j + count * trips
            count //= 2
        diag_block = qi & (blocks_per_tile - 1)
        diag_slot = n_loop & 1
        for c in range(blocks_per_tile):
            pl.when(diag_block == c)(functools.partial(update_diagonal, n_full, diag_slot, c))
        return carry

    lax.fori_loop(0, q_blocks, query_block, 0)


def _position_features(n_tiles, tk, p_rows):
    kpos = jnp.arange(n_tiles * tk, dtype=jnp.int32)[:, None]
    lane = jnp.arange(2 * DIFF_HD, dtype=jnp.int32)[None, :]
    masked = (kpos < META_PAD) | (kpos >= p_rows)
    feat = jnp.where(lane < FEAT_ROW, kpos // BLOCK,
           jnp.where(lane < FEAT_MASK, kpos % BLOCK,
           jnp.where(lane == FEAT_MASK, masked.astype(jnp.int32),
           jnp.where(lane < FEAT_QROW + FEAT_SPLIT, 1, 0))))
    return feat.astype(BF16).reshape(n_tiles, tk, 2 * DIFF_HD)


def _diff_attention(proj, lam_vecs, lam_init, subln_w, batch, p_rows):
    n = proj.shape[0]
    tq = BLOCK
    tk = ATTN_KEY_TILE
    heads = ATTN_HEADS_PER_STEP
    nq = p_rows // tq
    q_blocks = _largest_divisor(nq, (13, 5, 3, 1))
    steps = nq // q_blocks
    tr = q_blocks * tq
    hw = 2 * DIFF_HD
    k_tiles, v_t, first_tile = _kvprep(proj, batch, p_rows, tk)
    first_tile = first_tile[:, :nq, 0].reshape(batch * DIFF_HEADS * nq)
    n_tiles = k_tiles.shape[1]
    feat = _position_features(n_tiles, tk, p_rows)
    groups = DIFF_HEADS // heads
    grid_spec = pltpu.PrefetchScalarGridSpec(
        num_scalar_prefetch=1,
        grid=(batch, groups, steps),
        in_specs=[
            pl.BlockSpec((tr, heads * hw), lambda b, g, i, first: (b * steps + i, COL_DQ // (heads * hw) + g)),
            pl.BlockSpec((tq, heads * hw), lambda b, g, i, first: (
                b * nq + jnp.minimum((i + 1) * q_blocks, nq - 1), COL_DQ // (heads * hw) + g)),
            pl.BlockSpec((heads, n_tiles, tk, hw), lambda b, g, i, first: (b * groups + g, 0, 0, 0)),
            pl.BlockSpec((heads, n_tiles, VT_ROWS, tk), lambda b, g, i, first: (b * groups + g, 0, 0, 0)),
            pl.BlockSpec((n_tiles, tk, hw), lambda b, g, i, first: (0, 0, 0), pipeline_mode=pl.Buffered(1)),
            pl.BlockSpec((4, DIFF_HD), lambda b, g, i, first: (0, 0)),
            pl.BlockSpec((1, 1), lambda b, g, i, first: (0, 0)),
            pl.BlockSpec((DIFF_VD, 1), lambda b, g, i, first: (0, 0)),
        ],
        out_specs=pl.BlockSpec((tr, heads * DIFF_VD), lambda b, g, i, first: (b * steps + i, g)),
        scratch_shapes=[
            pltpu.VMEM((2, heads, tk, 2 * tq), F32),
            pltpu.VMEM((2, heads, 1, 2 * tq), F32),
            pltpu.VMEM((heads, 1, 2 * tq), F32),
            pltpu.VMEM((heads, VT_ROWS, 2 * tq), F32),
            pltpu.VMEM((heads, 2 * hw, 2 * tq), BF16),
        ],
    )
    return pl.pallas_call(
        functools.partial(_attn_kernel, tk=tk, heads=heads, q_blocks=q_blocks),
        grid_spec=grid_spec,
        out_shape=jax.ShapeDtypeStruct((n, D_MODEL), BF16),
        compiler_params=_params("arbitrary", "arbitrary", "arbitrary"),
        name="diffattn",
    )(first_tile, proj, proj, k_tiles, v_t, feat, lam_vecs, lam_init, subln_w)


def _merge_kernel(ret_ref, diff_ref, cb_ref, cc_ref, cx_ref, cch_ref, cxh_ref, g_ref, x_ref,
                  cw_ref, wb_ref, wo_ref, o_ref, *, tm, p_rows):
    start = pl.program_id(0) * tm
    rowi = lax.broadcasted_iota(jnp.int32, (tm, D_MODEL), 0)

    u = cc_ref[...].astype(F32) * cx_ref[...].astype(F32)
    uh = cch_ref[...].astype(F32) * cxh_ref[...].astype(F32)
    h1 = uh[BF16_SUBLANES - 1:BF16_SUBLANES, :]
    h2 = uh[BF16_SUBLANES - 2:BF16_SUBLANES - 1, :]
    u1 = jnp.where(rowi == 0, h1, pltpu.roll(u, 1, axis=0))
    u2 = jnp.where(rowi == 0, h2, jnp.where(rowi == 1, h1, pltpu.roll(u, 2, axis=0)))
    cw = cw_ref[...]
    conv = cb_ref[...].astype(F32) * (cw[0:1] * u2 + cw[1:2] * u1 + cw[2:3] * u)

    g = g_ref[...].astype(F32)
    g = 1.0 / (1.0 + jnp.exp(-g))
    merged = (g[:, 0:D_MODEL] * jnp.dot(ret_ref[...], wb_ref[0], preferred_element_type=F32)
              + g[:, D_MODEL:2 * D_MODEL] * jnp.dot(diff_ref[...], wb_ref[1], preferred_element_type=F32)
              + g[:, 2 * D_MODEL:] * jnp.dot(conv.astype(BF16), wb_ref[2], preferred_element_type=F32))
    y = x_ref[...] + jnp.dot(merged.astype(BF16), wo_ref[...], preferred_element_type=F32)
    r = rowi + (start - (start // p_rows) * p_rows)
    is_pad = (r < META_PAD) | ((r >= p_rows) & (r < p_rows + META_PAD))
    o_ref[...] = jnp.where(is_pad, 0.0, y)


def _merge(ret, diff, proj, x, conv_w, w_branch, w_out, p_rows, layer):
    n, d = x.shape
    tm = _largest_divisor(n, (512, 256, 128))
    halo_blocks = tm // BF16_SUBLANES

    def col(c):
        return lambda i: (i, c // d)

    def halo(c):
        return lambda i: (jnp.maximum(i * halo_blocks - 1, 0), c // d)

    return pl.pallas_call(
        functools.partial(_merge_kernel, tm=tm, p_rows=p_rows),
        grid=(n // tm,),
        in_specs=[
            pl.BlockSpec((tm, d), lambda i: (i, 0)),
            pl.BlockSpec((tm, d), lambda i: (i, 0)),
            pl.BlockSpec((tm, d), col(COL_CB)),
            pl.BlockSpec((tm, d), col(COL_CC)),
            pl.BlockSpec((tm, d), col(COL_CX)),
            pl.BlockSpec((BF16_SUBLANES, d), halo(COL_CC)),
            pl.BlockSpec((BF16_SUBLANES, d), halo(COL_CX)),
            pl.BlockSpec((tm, N_BRANCH * d), lambda i: (i, COL_GATE // (N_BRANCH * d))),
            pl.BlockSpec((tm, d), lambda i: (i, 0)),
            pl.BlockSpec((CONV_K, d), lambda i: (0, 0)),
            pl.BlockSpec((None, N_BRANCH, d, d), lambda i: (layer, 0, 0, 0), pipeline_mode=pl.Buffered(1)),
            pl.BlockSpec((None, d, d), lambda i: (layer, 0, 0), pipeline_mode=pl.Buffered(1)),
        ],
        out_specs=pl.BlockSpec((tm, d), lambda i: (i, 0)),
        out_shape=jax.ShapeDtypeStruct((n, d), F32),
        compiler_params=_params("parallel"),
        name="merge",
    )(ret, diff, proj, proj, proj, proj, proj, proj, x, conv_w, w_branch, w_out)


def _mlp_kernel(x_ref, nw_ref, wu_ref, wd_ref, o_ref, *, ff_chunk):
    x = x_ref[...]
    xn = _rms(x, nw_ref[...]).astype(BF16)
    acc = x
    for c in range(wu_ref.shape[1] // ff_chunk):
        cols = slice(c * ff_chunk, (c + 1) * ff_chunk)
        u = jnp.maximum(jnp.dot(xn, wu_ref[:, cols], preferred_element_type=F32), 0.0)
        acc = acc + jnp.dot((u * u).astype(BF16), wd_ref[cols, :], preferred_element_type=F32)
    o_ref[...] = acc


def _mlp(x, norm_w, w_up, w_down, layer):
    n, d = x.shape
    d_ff = w_up.shape[2]
    tm = _largest_divisor(n, (512, 256, 128))
    return pl.pallas_call(
        functools.partial(_mlp_kernel, ff_chunk=1024),
        grid=(n // tm,),
        in_specs=[
            pl.BlockSpec((tm, d), lambda i: (i, 0)),
            pl.BlockSpec((1, d), lambda i: (0, 0)),
            pl.BlockSpec((None, d, d_ff), lambda i: (layer, 0, 0)),
            pl.BlockSpec((None, d_ff, d), lambda i: (layer, 0, 0)),
        ],
        out_specs=pl.BlockSpec((tm, d), lambda i: (i, 0)),
        out_shape=jax.ShapeDtypeStruct((n, d), F32),
        compiler_params=_params("parallel"),
        name="mlp",
    )(x, norm_w, w_up, w_down)


def _final_norm_kernel(x_ref, w_ref, o_ref):
    o_ref[...] = _rms(x_ref[...], w_ref[...])


def _final_norm(h, w, batch, seq, p_rows):
    d = h.shape[1]
    tr = _largest_divisor(seq, (1024, 512, 256, 128))
    steps = seq // tr
    out = pl.pallas_call(
        _final_norm_kernel,
        grid=(batch, steps),
        in_specs=[
            pl.BlockSpec((pl.Element(tr), pl.Element(d)),
                         lambda b, i: (pl.multiple_of(b * p_rows + BLOCK + i * tr, BLOCK), 0)),
            pl.BlockSpec((1, d), lambda b, i: (0, 0)),
        ],
        out_specs=pl.BlockSpec((tr, d), lambda b, i: (b * steps + i, 0)),
        out_shape=jax.ShapeDtypeStruct((batch * seq, d), h.dtype),
        compiler_params=_params("parallel", "parallel"),
        name="final_norm",
    )(h, w)
    return out.reshape(batch, seq, d)


def kernel(x, meta_tokens, norm1_w, w_in, conv_w, diff_lambda, diff_subln_w, w_branch, w_out,
           norm2_w, w_up, w_down, final_norm_w):
    batch, seq, d = x.shape
    depth = w_in.shape[0]
    assert d == D_MODEL and w_in.shape[1:] == (D_MODEL, D_IN) and seq % BLOCK == 0
    p_rows = META_PAD + N_META + seq

    meta = jnp.broadcast_to(meta_tokens.astype(x.dtype)[None], (batch, N_META, d))
    h = jnp.concatenate([jnp.zeros((batch, META_PAD, d), x.dtype), meta, x], axis=1).reshape(batch * p_rows, d)

    layer_ids = jnp.arange(depth, dtype=F32)
    lam_inits = (0.8 - 0.6 * jnp.exp(-0.3 * layer_ids)).reshape(depth, 1, 1)
    column = jnp.arange(D_IN)
    w_in = (w_in * jnp.where((column >= COL_DQ) & (column < COL_DK), DIFF_HD ** -0.5 * LOG2E, 1.0)).astype(BF16)
    w_branch, w_out, w_up, w_down = (w.astype(BF16) for w in (w_branch, w_out, w_up, w_down))
    norm1_w, norm2_w = norm1_w.reshape(depth, 1, d), norm2_w.reshape(depth, 1, d)
    subln_w = diff_subln_w.reshape(depth, DIFF_VD, 1)
    lam_vecs = diff_lambda.astype(F32)

    for layer in range(depth):
        proj = _inproj(h, norm1_w[layer], w_in, layer)
        ret = _retention(proj, batch, p_rows)
        diff = _diff_attention(proj, lam_vecs[layer], lam_inits[layer], subln_w[layer], batch, p_rows)
        h = _merge(ret, diff, proj, h, conv_w[layer], w_branch, w_out, p_rows, layer)
        h = _mlp(h, norm2_w[layer], w_up, w_down, layer)
    return _final_norm(h, final_norm_w.reshape(1, d), batch, seq, p_rows)
```

```python
import functools
import math

import jax
import jax.numpy as jnp
from jax import lax
from jax.experimental import pallas as pl
from jax.experimental.pallas import tpu as pltpu

N_META = 16
BLOCK = 128
META_PAD = BLOCK - N_META
RET_HEADS = 4
RET_DK = 128
RET_DV = 256
DIFF_HEADS = 8
DIFF_HD = 64
DIFF_VD = 128
CONV_K = 3
N_BRANCH = 3
EPS = 1e-6
NEG_INF = -1e30

D_MODEL = RET_HEADS * RET_DV
COL_RQ, COL_RK, COL_RV, COL_RG = 0, 512, 1024, 2048
COL_DQ, COL_DK, COL_DV = 3072, 4096, 5120
COL_CB, COL_CC, COL_CX, COL_GATE = 6144, 7168, 8192, 9216
D_IN = COL_GATE + N_BRANCH * D_MODEL

V7X_VMEM_LIMIT_BYTES = 56 * 1024 * 1024
BF16_SUBLANES = 16
ATTN_KEY_TILE = 512
ATTN_MAX_TRIP_TILES = 4
ATTN_HEADS_PER_STEP = 4

F32 = jnp.float32
BF16 = jnp.bfloat16


def _largest_divisor(n, candidates):
    for c in candidates:
        if n % c == 0:
            return c
    raise ValueError(f"no tile in {candidates} divides {n}")


def _params(*semantics):
    return pltpu.CompilerParams(dimension_semantics=semantics, vmem_limit_bytes=V7X_VMEM_LIMIT_BYTES)


def _rms(x, w):
    return x * lax.rsqrt(jnp.mean(x * x, axis=-1, keepdims=True) + EPS) * w


def _inproj_kernel(x_ref, nw_ref, w_ref, o_ref, xn_ref):
    @pl.when(pl.program_id(1) == 0)
    def _():
        xn_ref[...] = _rms(x_ref[...], nw_ref[...]).astype(BF16)

    o_ref[...] = jnp.dot(xn_ref[...], w_ref[...], preferred_element_type=F32).astype(o_ref.dtype)


def _inproj(h, norm_w, w_in, layer):
    n, d = h.shape
    tm = _largest_divisor(n, (1024, 512, 256, 128))
    tn = 4096
    return pl.pallas_call(
        _inproj_kernel,
        grid=(n // tm, D_IN // tn),
        in_specs=[
            pl.BlockSpec((tm, d), lambda i, j: (i, 0)),
            pl.BlockSpec((1, d), lambda i, j: (0, 0)),
            pl.BlockSpec((None, d, tn), lambda i, j: (layer, 0, j)),
        ],
        out_specs=pl.BlockSpec((tm, tn), lambda i, j: (i, j)),
        out_shape=jax.ShapeDtypeStruct((n, D_IN), BF16),
        scratch_shapes=[pltpu.VMEM((tm, d), BF16)],
        compiler_params=_params("parallel", "arbitrary"),
        name="inproj",
    )(h, norm_w, w_in)


def _ret_log_gamma(head):
    return math.log1p(-(2.0 ** (-5.0 - head)))


def _retention_kernel(q_ref, k_ref, v_ref, g_ref, o_ref, state_ref, intra_ref, qd_ref, kd_ref, *, chunks):
    @pl.when(pl.program_id(1) == 0)
    def _():
        state_ref[...] = jnp.zeros_like(state_ref)
        scale = RET_DK ** -0.5
        i_sq = lax.broadcasted_iota(jnp.int32, (BLOCK, BLOCK), 0).astype(F32)
        j_sq = lax.broadcasted_iota(jnp.int32, (BLOCK, BLOCK), 1).astype(F32)
        i_dv = lax.broadcasted_iota(jnp.int32, (BLOCK, RET_DV), 0).astype(F32)
        dist = i_sq - j_sq
        for h in range(RET_HEADS):
            lg = _ret_log_gamma(h)
            intra_ref[h] = jnp.where(dist >= 0, jnp.exp(lg * jnp.maximum(dist, 0.0)), 0.0) * scale
            qd_ref[h] = jnp.exp(lg * (i_dv + 1.0))
            kd_ref[h] = jnp.exp(lg * (BLOCK - 1.0 - i_sq)) * scale

    for c in range(chunks):
        rows = pl.ds(c * BLOCK, BLOCK)
        for h in range(RET_HEADS):
            s_decay = math.exp(_ret_log_gamma(h) * BLOCK)
            q = q_ref[rows, h * RET_DK:(h + 1) * RET_DK]
            k = k_ref[rows, h * RET_DK:(h + 1) * RET_DK]
            v = v_ref[rows, h * RET_DV:(h + 1) * RET_DV]
            st = state_ref[h]
            scores = lax.dot_general(q, k, (((1,), (1,)), ((), ())), preferred_element_type=F32)
            scores = (scores * intra_ref[h]).astype(BF16)
            out = (jnp.dot(scores, v, preferred_element_type=F32)
                   + qd_ref[h] * jnp.dot(q, st.astype(BF16), preferred_element_type=F32))
            kdec = (k.astype(F32) * kd_ref[h]).astype(BF16)
            state_ref[h] = st * s_decay + lax.dot_general(
                kdec, v, (((0,), (0,)), ((), ())), preferred_element_type=F32)
            mu = jnp.mean(out, axis=-1, keepdims=True)
            cen = out - mu
            var = jnp.mean(cen * cen, axis=-1, keepdims=True)
            g = g_ref[rows, h * RET_DV:(h + 1) * RET_DV].astype(F32)
            silu = g / (1.0 + jnp.exp(-g))
            o_ref[rows, h * RET_DV:(h + 1) * RET_DV] = (cen * lax.rsqrt(var + EPS) * silu).astype(o_ref.dtype)


def _retention(proj, batch, p_rows):
    n = proj.shape[0]
    nc = p_rows // BLOCK
    chunks = _largest_divisor(nc, (13, 5, 3, 1))
    steps = nc // chunks
    tr = chunks * BLOCK
    qk_w = RET_HEADS * RET_DK

    def rows(b, c):
        return b * steps + c

    return pl.pallas_call(
        functools.partial(_retention_kernel, chunks=chunks),
        grid=(batch, steps),
        in_specs=[
            pl.BlockSpec((tr, qk_w), lambda b, c: (rows(b, c), COL_RQ // qk_w)),
            pl.BlockSpec((tr, qk_w), lambda b, c: (rows(b, c), COL_RK // qk_w)),
            pl.BlockSpec((tr, D_MODEL), lambda b, c: (rows(b, c), COL_RV // D_MODEL)),
            pl.BlockSpec((tr, D_MODEL), lambda b, c: (rows(b, c), COL_RG // D_MODEL)),
        ],
        out_specs=pl.BlockSpec((tr, D_MODEL), lambda b, c: (rows(b, c), 0)),
        out_shape=jax.ShapeDtypeStruct((n, D_MODEL), BF16),
        scratch_shapes=[
            pltpu.VMEM((RET_HEADS, RET_DK, RET_DV), F32),
            pltpu.VMEM((RET_HEADS, BLOCK, BLOCK), F32),
            pltpu.VMEM((RET_HEADS, BLOCK, RET_DV), F32),
            pltpu.VMEM((RET_HEADS, BLOCK, BLOCK), F32),
        ],
        compiler_params=_params("parallel", "arbitrary"),
        name="retention",
    )(proj, proj, proj, proj)


FEAT_BLK, FEAT_ROW, FEAT_MASK, FEAT_Q0, FEAT_QROW = 0, 3, 6, 7, 10
FEAT_SPLIT = 3
LOG2E = math.log2(math.e)
VT_ROWS = DIFF_VD + BF16_SUBLANES
SKIP_LOG2_MARGIN = 160.0
SKIP_NORM_SLACK = 1.01


def _max_sq_norm(x, map_selector):
    sq = x.astype(F32)
    sq = (sq * sq).astype(BF16)
    per_map = jnp.dot(sq, map_selector, preferred_element_type=F32)
    return jnp.max(jnp.max(per_map, axis=0, keepdims=True), axis=-1, keepdims=True)


def _first_live_tile(q2, k2, head, *, tk, n_tiles):
    blocks_per_tile = tk // BLOCK
    qi = lax.broadcasted_iota(jnp.int32, (BLOCK, BLOCK), 0)
    tj = lax.broadcasted_iota(jnp.int32, (BLOCK, BLOCK), 1)
    diag_tile = lax.shift_right_logical(qi, blocks_per_tile.bit_length() - 1)
    slope2 = jnp.exp2(-(jnp.zeros((BLOCK, BLOCK), F32) + head + 1.0)) * LOG2E
    kd2 = jnp.max(jnp.where(tj == diag_tile, k2, 0.0), axis=-1, keepdims=True)
    dist_min = (qi * BLOCK - tj * tk - (tk - 1)).astype(F32)
    bound = SKIP_NORM_SLACK * jnp.sqrt(q2) * (jnp.sqrt(k2) + jnp.sqrt(kd2)) + SKIP_LOG2_MARGIN
    dead = jnp.where((slope2 * dist_min > bound) & (tj < diag_tile), 1.0, 0.0)
    run = jnp.ones((BLOCK, 1), F32)
    count = jnp.zeros((BLOCK, 1), F32)
    for j in range(n_tiles):
        run = run * dead[:, j:j + 1]
        count = count + run
    return count


def _kvprep_kernel(q_ref, k_ref, v_ref, k_out_ref, vt_ref, js_ref, *, tk, n_real, n_tiles):
    blocks_per_tile = tk // BLOCK
    hw = 2 * DIFF_HD
    lane = lax.broadcasted_iota(jnp.int32, (BLOCK, hw), 1)
    row = lax.broadcasted_iota(jnp.int32, (BLOCK, hw), 0)
    map_selector = jnp.where(((lane == 0) & (row < DIFF_HD)) | ((lane == 1) & (row >= DIFF_HD)), 1.0, 0.0)
    map_selector = map_selector.astype(BF16)
    ones_row = jnp.where(lax.broadcasted_iota(jnp.int32, (VT_ROWS - DIFF_VD, BLOCK), 0) == 0, 1.0, 0.0)
    q2 = jnp.zeros((BLOCK, BLOCK), F32)
    k2 = jnp.zeros((BLOCK, BLOCK), F32)
    for t in range(n_tiles):
        for bi in range(blocks_per_tile):
            blk = t * blocks_per_tile + bi
            rows = pl.ds(bi * BLOCK, BLOCK)
            cols = pl.ds(bi * BLOCK, BLOCK)
            vt_ref[0, t, DIFF_VD:, cols] = ones_row.astype(BF16)
            if blk < n_real:
                src = slice(blk * BLOCK, (blk + 1) * BLOCK)
                kb = k_ref[src, :]
                k_out_ref[0, t, rows, :] = kb
                vt_ref[0, t, 0:DIFF_VD, cols] = v_ref[src, :].astype(F32).T.astype(BF16)
                q2 = jnp.where(row == blk, _max_sq_norm(q_ref[src, :], map_selector), q2)
                k2 = jnp.where(lane == t, jnp.maximum(k2, _max_sq_norm(kb, map_selector)), k2)
            else:
                k_out_ref[0, t, rows, :] = jnp.zeros((BLOCK, hw), BF16)
                vt_ref[0, t, 0:DIFF_VD, cols] = jnp.zeros((DIFF_VD, BLOCK), BF16)
    first = _first_live_tile(q2, k2, pl.program_id(1).astype(F32), tk=tk, n_tiles=n_tiles)
    js_ref[0] = jnp.broadcast_to(first, (BLOCK, BLOCK)).astype(jnp.int32)


def _kvprep(proj, batch, p_rows, tk):
    n_real = p_rows // BLOCK
    n_tiles = -(-p_rows // tk)
    hw = 2 * DIFF_HD
    blocks_per_tile = tk // BLOCK
    assert hw == BLOCK and n_real <= BLOCK and blocks_per_tile & (blocks_per_tile - 1) == 0
    return pl.pallas_call(
        functools.partial(_kvprep_kernel, tk=tk, n_real=n_real, n_tiles=n_tiles),
        grid=(batch, DIFF_HEADS),
        in_specs=[
            pl.BlockSpec((p_rows, hw), lambda b, h: (b, COL_DQ // hw + h)),
            pl.BlockSpec((p_rows, hw), lambda b, h: (b, COL_DK // hw + h)),
            pl.BlockSpec((p_rows, DIFF_VD), lambda b, h: (b, COL_DV // DIFF_VD + h)),
        ],
        out_specs=[
            pl.BlockSpec((1, n_tiles, tk, hw), lambda b, h: (b * DIFF_HEADS + h, 0, 0, 0)),
            pl.BlockSpec((1, n_tiles, VT_ROWS, tk), lambda b, h: (b * DIFF_HEADS + h, 0, 0, 0)),
            pl.BlockSpec((1, BLOCK, BLOCK), lambda b, h: (b * DIFF_HEADS + h, 0, 0)),
        ],
        out_shape=[
            jax.ShapeDtypeStruct((batch * DIFF_HEADS, n_tiles, tk, hw), BF16),
            jax.ShapeDtypeStruct((batch * DIFF_HEADS, n_tiles, VT_ROWS, tk), BF16),
            jax.ShapeDtypeStruct((batch * DIFF_HEADS, BLOCK, BLOCK), jnp.int32),
        ],
        compiler_params=_params("parallel", "parallel"),
        name="kvprep",
    )(proj, proj, proj)


def _attn_kernel(first_ref, q_ref, qn_ref, k_ref, vt_ref, feat_ref, lamv_ref, lami_ref, sw_ref, o_ref,
                 s_ref, smax_ref, m_ref, acc_ref, qcat_ref, coef_ref, *, tk, heads, q_blocks):
    tq = BLOCK
    hp = pl.program_id(1)
    lanes = 2 * tq
    hw = 2 * DIFF_HD
    blocks_per_tile = tk // BLOCK
    n_q_total = pl.num_programs(2) * q_blocks
    head0 = pl.program_id(0) * DIFF_HEADS + hp * heads

    lane_h = lax.broadcasted_iota(jnp.int32, (tq, hw), 1)
    row_q = lax.broadcasted_iota(jnp.int32, (tq, hw), 0).astype(F32)
    lv = lamv_ref[...]
    lam = (jnp.exp(jnp.sum(lv[0:1] * lv[1:2], axis=-1, keepdims=True))
           - jnp.exp(jnp.sum(lv[2:3] * lv[3:4], axis=-1, keepdims=True)) + lami_ref[...])

    def split_into(feat, first_lane, x):
        for piece in range(FEAT_SPLIT):
            head_part = x.astype(BF16).astype(F32)
            feat = jnp.where(lane_h == first_lane + piece, head_part, feat)
            x = x - head_part
        return feat

    def tile_range(qi):
        n_full = lax.shift_right_logical(qi, blocks_per_tile.bit_length() - 1)
        j_first = n_full
        for hh in range(heads):
            j_first = jnp.minimum(j_first, first_ref[(head0 + hh) * n_q_total + qi])
        return n_full, j_first

    def slope_log2(hh, shape):
        head = (hp * heads + hh).astype(F32)
        return jnp.exp2(-(jnp.zeros(shape, F32) + head + 1.0)) * LOG2E

    def block_independent_coefficients(hh):
        slope2 = slope_log2(hh, (tq, hw))
        feat = jnp.where(lane_h == FEAT_MASK, NEG_INF, 0.0)
        feat = split_into(feat, FEAT_BLK, slope2 * BLOCK)
        feat = split_into(feat, FEAT_ROW, slope2)
        feat = split_into(feat, FEAT_QROW, -slope2 * row_q)
        return jnp.concatenate([feat.T, feat.T], axis=1)

    coefficient_row = lax.broadcasted_iota(jnp.int32, (hw, lanes), 0)

    def qcat_of(hh, qi, q_block):
        bottom = coef_ref[hh]
        x = -slope_log2(hh, (1, lanes)) * (qi * tq).astype(F32)
        for piece in range(FEAT_SPLIT):
            head_part = x.astype(BF16).astype(F32)
            bottom = jnp.where(coefficient_row == FEAT_Q0 + piece, head_part, bottom)
            x = x - head_part
        q = q_block[:, hh * hw:(hh + 1) * hw].astype(F32)
        q_m0 = jnp.where(lane_h < DIFF_HD, q, 0.0)
        q_m1 = jnp.where(lane_h >= DIFF_HD, q, 0.0)
        top = jnp.concatenate([q_m0.T, q_m1.T], axis=1)
        return jnp.concatenate([top, bottom], axis=0).astype(BF16)

    def score_tile(j, slot, qcats):
        for hh in range(heads):
            k_aug = jnp.concatenate([k_ref[hh, j], feat_ref[j]], axis=1)
            s = jnp.dot(k_aug, qcats[hh], preferred_element_type=F32)
            s_ref[slot, hh] = s
            smax_ref[slot, hh] = jnp.max(s, axis=0, keepdims=True)

    def start_block(qi, q_block):
        _, j_first = tile_range(qi)
        qcats = [qcat_of(hh, qi, q_block) for hh in range(heads)]
        for hh in range(heads):
            qcat_ref[hh] = qcats[hh]
        score_tile(j_first, 0, qcats)

    @pl.when(pl.program_id(2) == 0)
    def _():
        for hh in range(heads):
            coef_ref[hh] = block_independent_coefficients(hh)
        start_block(pl.program_id(2) * q_blocks, q_ref[0:tq, :])

    def query_block(sub, carry):
        qi = pl.program_id(2) * q_blocks + sub
        q_rows = pl.ds(pl.multiple_of(sub * tq, tq), tq)
        n_full, j_first = tile_range(qi)
        qcats = [qcat_ref[hh] for hh in range(heads)]

        def scores(j, slot):
            score_tile(j, slot, qcats)

        def accumulate(hh, s, s_max, vt):
            m_old = m_ref[hh]
            m_new = jnp.maximum(m_old, s_max)
            alpha = jnp.exp2(m_old - m_new)
            p = jnp.exp2(s - m_new).astype(BF16)
            m_ref[hh] = m_new
            pv = jnp.dot(vt, p, preferred_element_type=F32)
            acc_ref[hh] = acc_ref[hh] * alpha + pv

        def update(j, slot):
            for hh in range(heads):
                accumulate(hh, s_ref[slot, hh], smax_ref[slot, hh], vt_ref[hh, j])

        def update_diagonal(j, slot, diag_block):
            live = (diag_block + 1) * BLOCK
            k_in_block = lax.broadcasted_iota(jnp.int32, (BLOCK, lanes), 0)
            q_in_block = lax.broadcasted_iota(jnp.int32, (BLOCK, lanes), 1)
            q_in_block = jnp.where(q_in_block >= tq, q_in_block - tq, q_in_block)
            for hh in range(heads):
                s = s_ref[slot, hh, 0:live, :]
                s_diag = jnp.where(k_in_block <= q_in_block, s[live - BLOCK:live], NEG_INF)
                if diag_block > 0:
                    s = jnp.concatenate([s[0:live - BLOCK], s_diag], axis=0)
                else:
                    s = s_diag
                accumulate(hh, s, jnp.max(s, axis=0, keepdims=True), vt_ref[hh, j, :, 0:live])
                o = acc_ref[hh, 0:DIFF_VD] / acc_ref[hh, DIFF_VD:DIFF_VD + 1]
                dlt = o[:, 0:tq] - lam * o[:, tq:lanes]
                ms = jnp.mean(dlt * dlt, axis=0, keepdims=True)
                y = dlt * lax.rsqrt(ms + EPS) * sw_ref[...] * (1.0 - lami_ref[...])
                o_ref[q_rows, hh * DIFF_VD:(hh + 1) * DIFF_VD] = y.T.astype(o_ref.dtype)
            next_rows = pl.ds(pl.multiple_of(jnp.minimum(sub + 1, q_blocks - 1) * tq, tq), tq)
            q_next = jnp.where(sub == q_blocks - 1, qn_ref[...], q_ref[next_rows, :])
            start_block(jnp.minimum(qi + 1, n_q_total - 1), q_next)

        m_ref[...] = jnp.full_like(m_ref, NEG_INF)
        acc_ref[...] = jnp.zeros_like(acc_ref)


        def tiles(j, count):
            for t in range(count):
                scores(j + t + 1, (t + 1) % 2)
                update(j + t, t % 2)

        n_loop = n_full - j_first
        j = j_first
        count = ATTN_MAX_TRIP_TILES
        while count >= 1:
            def body(i, c, base=j, count=count):
                tiles(base + count * i, count)
                return c
            if count == ATTN_MAX_TRIP_TILES:
                trips = lax.shift_right_logical(n_loop, count.bit_length() - 1)
            else:
                trips = lax.shift_right_logical(n_loop & count, count.bit_length() - 1)
            lax.fori_loop(0, trips, body, 0)
            j = j + count * trips
            count //= 2
        diag_block = qi & (blocks_per_tile - 1)
        diag_slot = n_loop & 1
        for c in range(blocks_per_tile):
            pl.when(diag_block == c)(functools.partial(update_diagonal, n_full, diag_slot, c))
        return carry

    lax.fori_loop(0, q_blocks, query_block, 0)


def _position_features(n_tiles, tk, p_rows):
    kpos = jnp.arange(n_tiles * tk, dtype=jnp.int32)[:, None]
    lane = jnp.arange(2 * DIFF_HD, dtype=jnp.int32)[None, :]
    masked = (kpos < META_PAD) | (kpos >= p_rows)
    feat = jnp.where(lane < FEAT_ROW, kpos // BLOCK,
           jnp.where(lane < FEAT_MASK, kpos % BLOCK,
           jnp.where(lane == FEAT_MASK, masked.astype(jnp.int32),
           jnp.where(lane < FEAT_QROW + FEAT_SPLIT, 1, 0))))
    return feat.astype(BF16).reshape(n_tiles, tk, 2 * DIFF_HD)


def _diff_attention(proj, lam_vecs, lam_init, subln_w, batch, p_rows):
    n = proj.shape[0]
    tq = BLOCK
    tk = ATTN_KEY_TILE
    heads = ATTN_HEADS_PER_STEP
    nq = p_rows // tq
    q_blocks = _largest_divisor(nq, (13, 5, 3, 1))
    steps = nq // q_blocks
    tr = q_blocks * tq
    hw = 2 * DIFF_HD
    k_tiles, v_t, first_tile = _kvprep(proj, batch, p_rows, tk)
    first_tile = first_tile[:, :nq, 0].reshape(batch * DIFF_HEADS * nq)
    n_tiles = k_tiles.shape[1]
    feat = _position_features(n_tiles, tk, p_rows)
    groups = DIFF_HEADS // heads
    grid_spec = pltpu.PrefetchScalarGridSpec(
        num_scalar_prefetch=1,
        grid=(batch, groups, steps),
        in_specs=[
            pl.BlockSpec((tr, heads * hw), lambda b, g, i, first: (b * steps + i, COL_DQ // (heads * hw) + g)),
            pl.BlockSpec((tq, heads * hw), lambda b, g, i, first: (
                b * nq + jnp.minimum((i + 1) * q_blocks, nq - 1), COL_DQ // (heads * hw) + g)),
            pl.BlockSpec((heads, n_tiles, tk, hw), lambda b, g, i, first: (b * groups + g, 0, 0, 0)),
            pl.BlockSpec((heads, n_tiles, VT_ROWS, tk), lambda b, g, i, first: (b * groups + g, 0, 0, 0)),
            pl.BlockSpec((n_tiles, tk, hw), lambda b, g, i, first: (0, 0, 0), pipeline_mode=pl.Buffered(1)),
            pl.BlockSpec((4, DIFF_HD), lambda b, g, i, first: (0, 0)),
            pl.BlockSpec((1, 1), lambda b, g, i, first: (0, 0)),
            pl.BlockSpec((DIFF_VD, 1), lambda b, g, i, first: (0, 0)),
        ],
        out_specs=pl.BlockSpec((tr, heads * DIFF_VD), lambda b, g, i, first: (b * steps + i, g)),
        scratch_shapes=[
            pltpu.VMEM((2, heads, tk, 2 * tq), F32),
            pltpu.VMEM((2, heads, 1, 2 * tq), F32),
            pltpu.VMEM((heads, 1, 2 * tq), F32),
            pltpu.VMEM((heads, VT_ROWS, 2 * tq), F32),
            pltpu.VMEM((heads, 2 * hw, 2 * tq), BF16),
            pltpu.VMEM((heads, hw, 2 * tq), F32),
        ],
    )
    return pl.pallas_call(
        functools.partial(_attn_kernel, tk=tk, heads=heads, q_blocks=q_blocks),
        grid_spec=grid_spec,
        out_shape=jax.ShapeDtypeStruct((n, D_MODEL), BF16),
        compiler_params=_params("arbitrary", "arbitrary", "arbitrary"),
        name="diffattn",
    )(first_tile, proj, proj, k_tiles, v_t, feat, lam_vecs, lam_init, subln_w)


def _merge_kernel(ret_ref, diff_ref, cb_ref, cc_ref, cx_ref, cch_ref, cxh_ref, g_ref, x_ref,
                  cw_ref, wb_ref, wo_ref, o_ref, *, tm, p_rows):
    start = pl.program_id(0) * tm
    rowi = lax.broadcasted_iota(jnp.int32, (tm, D_MODEL), 0)

    u = cc_ref[...].astype(F32) * cx_ref[...].astype(F32)
    uh = cch_ref[...].astype(F32) * cxh_ref[...].astype(F32)
    h1 = uh[BF16_SUBLANES - 1:BF16_SUBLANES, :]
    h2 = uh[BF16_SUBLANES - 2:BF16_SUBLANES - 1, :]
    u1 = jnp.where(rowi == 0, h1, pltpu.roll(u, 1, axis=0))
    u2 = jnp.where(rowi == 0, h2, jnp.where(rowi == 1, h1, pltpu.roll(u, 2, axis=0)))
    cw = cw_ref[...]
    conv = cb_ref[...].astype(F32) * (cw[0:1] * u2 + cw[1:2] * u1 + cw[2:3] * u)

    g = g_ref[...].astype(F32)
    g = 1.0 / (1.0 + jnp.exp(-g))
    merged = (g[:, 0:D_MODEL] * jnp.dot(ret_ref[...], wb_ref[0], preferred_element_type=F32)
              + g[:, D_MODEL:2 * D_MODEL] * jnp.dot(diff_ref[...], wb_ref[1], preferred_element_type=F32)
              + g[:, 2 * D_MODEL:] * jnp.dot(conv.astype(BF16), wb_ref[2], preferred_element_type=F32))
    y = x_ref[...] + jnp.dot(merged.astype(BF16), wo_ref[...], preferred_element_type=F32)
    r = rowi + (start - (start // p_rows) * p_rows)
    is_pad = (r < META_PAD) | ((r >= p_rows) & (r < p_rows + META_PAD))
    o_ref[...] = jnp.where(is_pad, 0.0, y)


def _merge(ret, diff, proj, x, conv_w, w_branch, w_out, p_rows, layer):
    n, d = x.shape
    tm = _largest_divisor(n, (512, 256, 128))
    halo_blocks = tm // BF16_SUBLANES

    def col(c):
        return lambda i: (i, c // d)

    def halo(c):
        return lambda i: (jnp.maximum(i * halo_blocks - 1, 0), c // d)

    return pl.pallas_call(
        functools.partial(_merge_kernel, tm=tm, p_rows=p_rows),
        grid=(n // tm,),
        in_specs=[
            pl.BlockSpec((tm, d), lambda i: (i, 0)),
            pl.BlockSpec((tm, d), lambda i: (i, 0)),
            pl.BlockSpec((tm, d), col(COL_CB)),
            pl.BlockSpec((tm, d), col(COL_CC)),
            pl.BlockSpec((tm, d), col(COL_CX)),
            pl.BlockSpec((BF16_SUBLANES, d), halo(COL_CC)),
            pl.BlockSpec((BF16_SUBLANES, d), halo(COL_CX)),
            pl.BlockSpec((tm, N_BRANCH * d), lambda i: (i, COL_GATE // (N_BRANCH * d))),
            pl.BlockSpec((tm, d), lambda i: (i, 0)),
            pl.BlockSpec((CONV_K, d), lambda i: (0, 0)),
            pl.BlockSpec((None, N_BRANCH, d, d), lambda i: (layer, 0, 0, 0), pipeline_mode=pl.Buffered(1)),
            pl.BlockSpec((None, d, d), lambda i: (layer, 0, 0), pipeline_mode=pl.Buffered(1)),
        ],
        out_specs=pl.BlockSpec((tm, d), lambda i: (i, 0)),
        out_shape=jax.ShapeDtypeStruct((n, d), F32),
        compiler_params=_params("parallel"),
        name="merge",
    )(ret, diff, proj, proj, proj, proj, proj, proj, x, conv_w, w_branch, w_out)


def _mlp_kernel(x_ref, nw_ref, wu_ref, wd_ref, o_ref, *, ff_chunk):
    x = x_ref[...]
    xn = _rms(x, nw_ref[...]).astype(BF16)
    acc = x
    for c in range(wu_ref.shape[1] // ff_chunk):
        cols = slice(c * ff_chunk, (c + 1) * ff_chunk)
        u = jnp.maximum(jnp.dot(xn, wu_ref[:, cols], preferred_element_type=F32), 0.0)
        acc = acc + jnp.dot((u * u).astype(BF16), wd_ref[cols, :], preferred_element_type=F32)
    o_ref[...] = acc


def _mlp(x, norm_w, w_up, w_down, layer):
    n, d = x.shape
    d_ff = w_up.shape[2]
    tm = _largest_divisor(n, (512, 256, 128))
    return pl.pallas_call(
        functools.partial(_mlp_kernel, ff_chunk=1024),
        grid=(n // tm,),
        in_specs=[
            pl.BlockSpec((tm, d), lambda i: (i, 0)),
            pl.BlockSpec((1, d), lambda i: (0, 0)),
            pl.BlockSpec((None, d, d_ff), lambda i: (layer, 0, 0)),
            pl.BlockSpec((None, d_ff, d), lambda i: (layer, 0, 0)),
        ],
        out_specs=pl.BlockSpec((tm, d), lambda i: (i, 0)),
        out_shape=jax.ShapeDtypeStruct((n, d), F32),
        compiler_params=_params("parallel"),
        name="mlp",
    )(x, norm_w, w_up, w_down)


def _final_norm_kernel(x_ref, w_ref, o_ref):
    o_ref[...] = _rms(x_ref[...], w_ref[...])


def _final_norm(h, w, batch, seq, p_rows):
    d = h.shape[1]
    tr = _largest_divisor(seq, (1024, 512, 256, 128))
    steps = seq // tr
    out = pl.pallas_call(
        _final_norm_kernel,
        grid=(batch, steps),
        in_specs=[
            pl.BlockSpec((pl.Element(tr), pl.Element(d)),
                         lambda b, i: (pl.multiple_of(b * p_rows + BLOCK + i * tr, BLOCK), 0)),
            pl.BlockSpec((1, d), lambda b, i: (0, 0)),
        ],
        out_specs=pl.BlockSpec((tr, d), lambda b, i: (b * steps + i, 0)),
        out_shape=jax.ShapeDtypeStruct((batch * seq, d), h.dtype),
        compiler_params=_params("parallel", "parallel"),
        name="final_norm",
    )(h, w)
    return out.reshape(batch, seq, d)


def kernel(x, meta_tokens, norm1_w, w_in, conv_w, diff_lambda, diff_subln_w, w_branch, w_out,
           norm2_w, w_up, w_down, final_norm_w):
    batch, seq, d = x.shape
    depth = w_in.shape[0]
    assert d == D_MODEL and w_in.shape[1:] == (D_MODEL, D_IN) and seq % BLOCK == 0
    p_rows = META_PAD + N_META + seq

    meta = jnp.broadcast_to(meta_tokens.astype(x.dtype)[None], (batch, N_META, d))
    h = jnp.concatenate([jnp.zeros((batch, META_PAD, d), x.dtype), meta, x], axis=1).reshape(batch * p_rows, d)

    layer_ids = jnp.arange(depth, dtype=F32)
    lam_inits = (0.8 - 0.6 * jnp.exp(-0.3 * layer_ids)).reshape(depth, 1, 1)
    column = jnp.arange(D_IN)
    w_in = (w_in * jnp.where((column >= COL_DQ) & (column < COL_DK), DIFF_HD ** -0.5 * LOG2E, 1.0)).astype(BF16)
    w_branch, w_out, w_up, w_down = (w.astype(BF16) for w in (w_branch, w_out, w_up, w_down))
    norm1_w, norm2_w = norm1_w.reshape(depth, 1, d), norm2_w.reshape(depth, 1, d)
    subln_w = diff_subln_w.reshape(depth, DIFF_VD, 1)
    lam_vecs = diff_lambda.astype(F32)

    for layer in range(depth):
        proj = _inproj(h, norm1_w[layer], w_in, layer)
        ret = _retention(proj, batch, p_rows)
        diff = _diff_attention(proj, lam_vecs[layer], lam_inits[layer], subln_w[layer], batch, p_rows)
        h = _merge(ret, diff, proj, h, conv_w[layer], w_branch, w_out, p_rows, layer)
        h = _mlp(h, norm2_w[layer], w_up, w_down, layer)
    return _final_norm(h, final_norm_w.reshape(1, d), batch, seq, p_rows)
```
